```python
import math
import jax, jax.numpy as jnp
from jax import lax
import numpy as np

D_MODEL = 1024
BATCH = 8
SEQ = 2048
DEPTH = 1

CHUNK = 64
Q_BLOCK = 128

DN_HEADS = 4
DN_HEAD_DIM = 128
DN_WIDTH = DN_HEADS * DN_HEAD_DIM
CONV_WIDTH = 4

SB_HEADS = 8
SB_HEAD_DIM = 64
SB_WIDTH = SB_HEADS * SB_HEAD_DIM

N_EXPERTS = 256
TOP_K = 8
N_GROUPS = 8
TOP_GROUPS = 4
EXPERT_FF = 256
SHARED_FF = 256
ROUTED_SCALE = 2.5
EXPERT_BLOCK = 128

DEEPNORM_ALPHA = (2 * DEPTH) ** 0.25
DEEPNORM_BETA = (8 * DEPTH) ** -0.25

LN_EPS = 1e-5
NORM_EPS = 1e-6

PROJ_SPLITS = (DN_WIDTH, DN_WIDTH, DN_WIDTH, DN_WIDTH, DN_HEADS, DN_HEADS,
               SB_WIDTH, SB_WIDTH, SB_WIDTH, D_MODEL, D_MODEL)
PROJ_WIDTH = sum(PROJ_SPLITS)

kernel_name = "hybrid_deltanet_stickbreaking_moe_deepnorm"


def layer_norm(x, g, b):
    xf = x.astype(jnp.float32)
    mu = jnp.mean(xf, -1, keepdims=True)
    var = jnp.mean(jnp.square(xf - mu), -1, keepdims=True)
    return ((xf - mu) * lax.rsqrt(var + LN_EPS) * g.astype(jnp.float32)
            + b.astype(jnp.float32)).astype(x.dtype)


def rms_norm(u, g):
    uf = u.astype(jnp.float32)
    return uf * lax.rsqrt(jnp.mean(uf * uf, -1, keepdims=True) + NORM_EPS) * g.astype(jnp.float32)


def l2_normalize(u):
    return u * lax.rsqrt(jnp.sum(u * u, -1, keepdims=True) + NORM_EPS)


def causal_depthwise_conv(u, w):
    S = u.shape[1]
    up = jnp.pad(u, ((0, 0), (CONV_WIDTH - 1, 0), (0, 0)))
    out = up[:, 0:S] * w[0]
    for tap in range(1, CONV_WIDTH):
        out = out + up[:, tap:tap + S] * w[tap]
    return out


def split_heads(t, n_heads):
    B, S, _ = t.shape
    return t.reshape(B, S, n_heads, -1).transpose(0, 2, 1, 3)


def gated_delta_rule(q, k, v, g, beta):
    B, H, S, dk = q.shape
    dv = v.shape[-1]
    N = S // CHUNK
    q = q * (dk ** -0.5)
    q = q.reshape(B, H, N, CHUNK, dk)
    k = k.reshape(B, H, N, CHUNK, dk)
    v = v.reshape(B, H, N, CHUNK, dv)
    g = g.reshape(B, H, N, CHUNK)
    beta = beta.reshape(B, H, N, CHUNK)
    G = jnp.cumsum(g, -1)
    pos = jnp.arange(CHUNK)
    incl = pos[:, None] >= pos[None, :]
    strict = pos[:, None] > pos[None, :]
    diff = G[..., :, None] - G[..., None, :]
    decay = jnp.where(incl, jnp.exp(jnp.where(incl, diff, 0.0)), 0.0)
    k_beta = k * beta[..., None]
    v_beta = v * beta[..., None]
    m = jnp.where(strict, jnp.einsum('bhnid,bhnjd->bhnij', k_beta, k) * decay, 0.0)
    eye = jnp.eye(CHUNK, dtype=q.dtype)
    rhs = jnp.concatenate([v_beta, k_beta * jnp.exp(G)[..., None]], -1)
    sol = lax.linalg.triangular_solve(eye + m, rhs, left_side=True, lower=True,
                                      unit_diagonal=True)
    u, w = sol[..., :dv], sol[..., dv:]
    attn = jnp.einsum('bhnid,bhnjd->bhnij', q, k) * decay
    q_dec = q * jnp.exp(G)[..., None]
    k_dec = k * jnp.exp(G[..., -1:] - G)[..., None]
    last = jnp.exp(G[..., -1])

    def step(state, inp):
        u_c, w_c, attn_c, qd_c, kd_c, last_c = inp
        v_new = u_c - jnp.einsum('bhcd,bhde->bhce', w_c, state)
        o = (jnp.einsum('bhcd,bhde->bhce', qd_c, state)
             + jnp.einsum('bhij,bhje->bhie', attn_c, v_new))
        state = state * last_c[..., None, None] + jnp.einsum('bhcd,bhce->bhde', kd_c, v_new)
        return state, o

    xs = (jnp.moveaxis(u, 2, 0), jnp.moveaxis(w, 2, 0), jnp.moveaxis(attn, 2, 0),
          jnp.moveaxis(q_dec, 2, 0), jnp.moveaxis(k_dec, 2, 0), jnp.moveaxis(last, 2, 0))
    s0 = jnp.zeros((B, H, dk, dv), jnp.float32)
    _, o = lax.scan(step, s0, xs)
    return jnp.moveaxis(o, 0, 2).reshape(B, H, S, dv)


def stick_breaking_attention(q, k, v):
    S = q.shape[2]
    scale = q.shape[-1] ** -0.5
    outs = []
    for blk in range(S // Q_BLOCK):
        q0 = blk * Q_BLOCK
        L = q0 + Q_BLOCK
        qb = q[:, :, q0:L].astype(jnp.float32)
        kb = k[:, :, :L].astype(jnp.float32)
        vb = v[:, :, :L].astype(jnp.float32)
        z = jnp.einsum('bhqd,bhkd->bhqk', qb, kb) * scale
        qpos = q0 + jnp.arange(Q_BLOCK)
        kpos = jnp.arange(L)
        mask = kpos[None, :] < qpos[:, None]
        log_beta = jax.nn.log_sigmoid(z)
        log_rest = jnp.where(mask, jax.nn.log_sigmoid(-z), 0.0)
        between = lax.cumsum(log_rest, axis=3, reverse=True) - log_rest
        a = jnp.where(mask, jnp.exp(log_beta + between), 0.0)
        outs.append(jnp.einsum('bhqk,bhkd->bhqd', a, vb))
    return jnp.concatenate(outs, 2)


def swiglu(u, w_up, w_down):
    gate, up = jnp.split(u @ w_up, 2, -1)
    return (jax.nn.silu(gate) * up) @ w_down


def route(h, w_router, router_bias):
    T = h.shape[0]
    scores = jax.nn.sigmoid((h @ w_router).astype(jnp.float32))
    sel = scores + router_bias.astype(jnp.float32)
    grouped = sel.reshape(T, N_GROUPS, N_EXPERTS // N_GROUPS)
    group_score = jnp.sum(lax.top_k(grouped, 2)[0], -1)
    _, top_groups = lax.top_k(group_score, TOP_GROUPS)
    group_mask = jnp.any(top_groups[:, :, None] == jnp.arange(N_GROUPS)[None, None, :], axis=1)
    expert_mask = jnp.repeat(group_mask, N_EXPERTS // N_GROUPS, axis=1)
    _, idx = lax.top_k(jnp.where(expert_mask, sel, -jnp.inf), TOP_K)
    w = jnp.take_along_axis(scores, idx, -1)
    w = w / jnp.sum(w, -1, keepdims=True) * ROUTED_SCALE
    return idx, w


def routed_experts(h, idx, gate, w_up, w_down):
    T, D = h.shape
    M = T * TOP_K
    flat_e = idx.reshape(M).astype(jnp.int32)
    flat_tok = jnp.arange(M, dtype=jnp.int32) // TOP_K
    flat_gate = gate.reshape(M)
    order = jnp.argsort(flat_e)
    se, stok, sgate = flat_e[order], flat_tok[order], flat_gate[order]
    counts = jnp.zeros((N_EXPERTS,), jnp.int32).at[flat_e].add(1)
    start = jnp.cumsum(counts) - counts
    padded = (counts + EXPERT_BLOCK - 1) // EXPERT_BLOCK * EXPERT_BLOCK
    pend = jnp.cumsum(padded)
    pstart = pend - padded
    dest = pstart[se] + jnp.arange(M, dtype=jnp.int32) - start[se]
    n_blocks = -(-(M + N_EXPERTS * (EXPERT_BLOCK - 1)) // EXPERT_BLOCK)
    P = n_blocks * EXPERT_BLOCK
    pad_tok = jnp.full((P,), T, jnp.int32).at[dest].set(stok)
    pad_gate = jnp.zeros((P,), jnp.float32).at[dest].set(sgate)
    block_start = jnp.arange(n_blocks, dtype=jnp.int32) * EXPERT_BLOCK
    block_e = jnp.minimum(jnp.sum(pend[None, :] <= block_start[:, None], 1), N_EXPERTS - 1)
    h_pad = jnp.concatenate([h, jnp.zeros((1, D), h.dtype)], 0)

    def run_block(args):
        tok, e = args
        return swiglu(h_pad[tok], w_up[e], w_down[e])

    y = lax.map(run_block, (pad_tok.reshape(n_blocks, EXPERT_BLOCK), block_e))
    y = y.reshape(P, D).astype(jnp.float32) * pad_gate[:, None]
    return jax.ops.segment_sum(y, pad_tok, num_segments=T + 1)[:T].astype(h.dtype)


def hybrid_layer(x, w_in, conv_w, dn_a_log, dn_dt_bias, dn_norm_g, w_proj_a, w_proj_b,
                 w_out, ln1_g, ln1_b, w_router, router_bias, w_shared_up, w_shared_down,
                 w_expert_up, w_expert_down, ln2_g, ln2_b):
    B, S, D = x.shape
    proj = x @ w_in
    cuts = [int(c) for c in np.cumsum(PROJ_SPLITS)[:-1]]
    dq, dk, dv, dz, dbeta, da, sq, sk, sv, gate_a, gate_b = jnp.split(proj, cuts, axis=-1)

    qkv = jax.nn.silu(causal_depthwise_conv(jnp.concatenate([dq, dk, dv], -1), conv_w))
    dq, dk, dv = jnp.split(qkv.astype(jnp.float32), 3, -1)
    q_a = l2_normalize(split_heads(dq, DN_HEADS))
    k_a = l2_normalize(split_heads(dk, DN_HEADS))
    v_a = split_heads(dv, DN_HEADS)
    log_decay = (-jnp.exp(dn_a_log.astype(jnp.float32))
                 * jax.nn.softplus(da.astype(jnp.float32) + dn_dt_bias.astype(jnp.float32)))
    beta = jax.nn.sigmoid(dbeta.astype(jnp.float32))
    o_a = gated_delta_rule(q_a, k_a, v_a, log_decay.transpose(0, 2, 1), beta.transpose(0, 2, 1))
    o_a = o_a.transpose(0, 2, 1, 3)
    o_a = rms_norm(o_a, dn_norm_g) * jax.nn.silu(
        dz.astype(jnp.float32).reshape(B, S, DN_HEADS, DN_HEAD_DIM))
    y_a = o_a.reshape(B, S, DN_WIDTH).astype(x.dtype) @ w_proj_a

    o_b = stick_breaking_attention(split_heads(sq, SB_HEADS), split_heads(sk, SB_HEADS),
                                   split_heads(sv, SB_HEADS))
    y_b = o_b.transpose(0, 2, 1, 3).reshape(B, S, SB_WIDTH).astype(x.dtype) @ w_proj_b

    merged = jax.nn.sigmoid(gate_a) * y_a + jax.nn.sigmoid(gate_b) * y_b
    h = layer_norm(DEEPNORM_ALPHA * x + merged @ w_out, ln1_g, ln1_b)

    hf = h.reshape(B * S, D)
    idx, gates = route(hf, w_router, router_bias)
    f = swiglu(hf, w_shared_up, w_shared_down) + routed_experts(hf, idx, gates, w_expert_up,
                                                                w_expert_down)
    return layer_norm(DEEPNORM_ALPHA * h + f.reshape(B, S, D), ln2_g, ln2_b)


def setup_inputs(seed: int = 0) -> dict:
    key = jax.random.key(seed)
    ks = jax.random.split(key, 20)
    f32 = jnp.float32
    nrm = lambda k, shape, scale: jax.random.normal(k, shape, f32) * scale
    dt = jnp.exp(jax.random.uniform(ks[4], (DEPTH, DN_HEADS), f32, math.log(1e-3), math.log(1e-1)))
    return {
        "x": nrm(ks[0], (BATCH, SEQ, D_MODEL), 1.0),
        "w_in": nrm(ks[1], (DEPTH, D_MODEL, PROJ_WIDTH), D_MODEL ** -0.5),
        "conv_w": nrm(ks[2], (DEPTH, CONV_WIDTH, 3 * DN_WIDTH), CONV_WIDTH ** -0.5),
        "dn_a_log": jnp.log(jax.random.uniform(ks[3], (DEPTH, DN_HEADS), f32, 1.0, 16.0)),
        "dn_dt_bias": dt + jnp.log(-jnp.expm1(-dt)),
        "dn_norm_g": 1.0 + nrm(ks[5], (DEPTH, DN_HEAD_DIM), 0.02),
        "w_proj_a": nrm(ks[6], (DEPTH, DN_WIDTH, D_MODEL), DN_WIDTH ** -0.5),
        "w_proj_b": nrm(ks[7], (DEPTH, SB_WIDTH, D_MODEL), SB_WIDTH ** -0.5),
        "w_out": nrm(ks[8], (DEPTH, D_MODEL, D_MODEL), D_MODEL ** -0.5 * DEEPNORM_BETA),
        "ln1_g": 1.0 + nrm(ks[9], (DEPTH, D_MODEL), 0.02),
        "ln1_b": nrm(ks[10], (DEPTH, D_MODEL), 0.02),
        "w_router": nrm(ks[11], (DEPTH, D_MODEL, N_EXPERTS), D_MODEL ** -0.5),
        "router_bias": nrm(ks[12], (DEPTH, N_EXPERTS), 0.01),
        "w_shared_up": nrm(ks[13], (DEPTH, D_MODEL, 2 * SHARED_FF), D_MODEL ** -0.5),
        "w_shared_down": nrm(ks[14], (DEPTH, SHARED_FF, D_MODEL), SHARED_FF ** -0.5 * DEEPNORM_BETA),
        "w_expert_up": nrm(ks[15], (DEPTH, N_EXPERTS, D_MODEL, 2 * EXPERT_FF), D_MODEL ** -0.5),
        "w_expert_down": nrm(ks[16], (DEPTH, N_EXPERTS, EXPERT_FF, D_MODEL),
                             EXPERT_FF ** -0.5 * DEEPNORM_BETA),
        "ln2_g": 1.0 + nrm(ks[17], (DEPTH, D_MODEL), 0.02),
        "ln2_b": nrm(ks[18], (DEPTH, D_MODEL), 0.02),
    }


def reference(x, w_in, conv_w, dn_a_log, dn_dt_bias, dn_norm_g, w_proj_a, w_proj_b, w_out,
              ln1_g, ln1_b, w_router, router_bias, w_shared_up, w_shared_down, w_expert_up,
              w_expert_down, ln2_g, ln2_b):
    for layer in range(DEPTH):
        x = hybrid_layer(x, w_in[layer], conv_w[layer], dn_a_log[layer], dn_dt_bias[layer],
                         dn_norm_g[layer], w_proj_a[layer], w_proj_b[layer], w_out[layer],
                         ln1_g[layer], ln1_b[layer], w_router[layer], router_bias[layer],
                         w_shared_up[layer], w_shared_down[layer], w_expert_up[layer],
                         w_expert_down[layer], ln2_g[layer], ln2_b[layer])
    return x
```

```python
import functools
import math

import jax
import jax.numpy as jnp
import numpy as np
from jax import lax
from jax.experimental import pallas as pl
from jax.experimental.pallas import tpu as pltpu

F32 = jnp.float32
BF16 = jnp.bfloat16
I32 = jnp.int32

D_MODEL = 1024
CHUNK = 64
DN_HEADS = 4
DN_HEAD_DIM = 128
DN_WIDTH = DN_HEADS * DN_HEAD_DIM
CONV_WIDTH = 4
SB_HEADS = 8
SB_HEAD_DIM = 64
SB_WIDTH = SB_HEADS * SB_HEAD_DIM
N_EXPERTS = 256
TOP_K = 8
N_GROUPS = 8
GROUP_SIZE = N_EXPERTS // N_GROUPS
TOP_GROUPS = 4
EXPERT_FF = 256
SHARED_FF = 256
ROUTED_SCALE = 2.5
EXPERT_BLOCK = 128
DEPTH = 1
DEEPNORM_ALPHA = (2 * DEPTH) ** 0.25
LN_EPS = 1e-5
NORM_EPS = 1e-6

LANES = 128
SUBLANES = 8
VMEM_LIMIT = 56 * 1024 * 1024

_C_QKV = (0, 3 * DN_WIDTH)
_C_Z = (_C_QKV[1], _C_QKV[1] + DN_WIDTH)
_C_BA = (_C_Z[1], _C_Z[1] + LANES)
_C_SQ = (_C_BA[1], _C_BA[1] + SB_WIDTH)
_C_SK = (_C_SQ[1], _C_SQ[1] + SB_WIDTH)
_C_SV = (_C_SK[1], _C_SK[1] + SB_WIDTH)
_C_GA = (_C_SV[1], _C_SV[1] + D_MODEL)
_C_GB = (_C_GA[1], _C_GA[1] + D_MODEL)
PROJ_PACKED = _C_GB[1]


def _params(sem, vmem=VMEM_LIMIT):
    return pltpu.CompilerParams(dimension_semantics=sem, vmem_limit_bytes=vmem)


def _const_spec(shape):
    return pl.BlockSpec(shape, lambda *_: (0,) * len(shape), pipeline_mode=pl.Buffered(1))


def _dot(a, b):
    return jnp.dot(a, b, preferred_element_type=F32)


def _dot_nt(a, b):
    return lax.dot_general(a, b, (((1,), (1,)), ((), ())), preferred_element_type=F32)


def _split_bf16(a):
    hi = a.astype(BF16)
    return hi, (a - hi.astype(F32)).astype(BF16)


def _proj_kernel(x_ref, w_ref, qkv_ref, z_ref, ba_ref, sq_ref, sk_ref, sv_ref, ga_ref, gb_ref):
    xb = x_ref[...].astype(BF16)

    def mm(c):
        return _dot(xb, w_ref[:, c[0]:c[1]])

    qkv_ref[...] = mm(_C_QKV)
    z_ref[...] = mm(_C_Z)
    ba_ref[...] = mm(_C_BA)
    sq_ref[...] = mm(_C_SQ).astype(BF16)
    sk_ref[...] = mm(_C_SK).astype(BF16)
    sv_ref[...] = mm(_C_SV).astype(BF16)
    ga_ref[...] = mm(_C_GA)
    gb_ref[...] = mm(_C_GB)


def _input_projection(xf, w_packed, tm=256):
    T = xf.shape[0]
    widths = [(3 * DN_WIDTH, F32), (DN_WIDTH, F32), (LANES, F32), (SB_WIDTH, BF16), (SB_WIDTH, BF16),
              (SB_WIDTH, BF16), (D_MODEL, F32), (D_MODEL, F32)]
    return pl.pallas_call(
        _proj_kernel,
        out_shape=[jax.ShapeDtypeStruct((T, w), dt) for w, dt in widths],
        grid=(T // tm,),
        in_specs=[pl.BlockSpec((tm, D_MODEL), lambda i: (i, 0)), _const_spec((D_MODEL, PROJ_PACKED))],
        out_specs=[pl.BlockSpec((tm, w), lambda i: (i, 0)) for w, _ in widths],
        compiler_params=_params(("parallel",)),
        name="input_projection",
    )(xf, w_packed)


def _mm3(a, b):
    ah, al = _split_bf16(a)
    bh, bl = _split_bf16(b)
    return _dot(ah, bh) + _dot(ah, bl) + _dot(al, bh)


def _deltanet_kernel(qkv_ref, z_ref, ba_ref, cw_ref, alog_ref, dtb_ref, ng_ref, o_ref, win_ref, st_ref):
    n = pl.program_id(1)
    C = CHUNK
    hd = DN_HEAD_DIM

    @pl.when(n == 0)
    def _():
        win_ref[0:SUBLANES, :] = jnp.zeros((SUBLANES, 3 * DN_WIDTH), F32)
        st_ref[...] = jnp.zeros_like(st_ref)

    win_ref[SUBLANES:SUBLANES + C, :] = qkv_ref[...]
    conv = win_ref[pl.ds(SUBLANES - (CONV_WIDTH - 1), C), :] * cw_ref[0:1, :]
    for tap in range(1, CONV_WIDTH):
        conv = conv + win_ref[pl.ds(SUBLANES - (CONV_WIDTH - 1) + tap, C), :] * cw_ref[tap:tap + 1, :]
    win_ref[0:SUBLANES, :] = win_ref[C:C + SUBLANES, :]
    act = _silu(conv)

    bt = ba_ref[...]
    beta_all = jax.nn.sigmoid(bt)
    g_all = -jnp.exp(alog_ref[...]) * jax.nn.softplus(bt + dtb_ref[...])
    r_io = lax.broadcasted_iota(I32, (C, C), 0)
    c_io = lax.broadcasted_iota(I32, (C, C), 1)
    incl = r_io >= c_io
    strict = r_io > c_io
    tri = jnp.where(incl, 1.0, 0.0).astype(BF16)
    g_hi, g_lo = _split_bf16(g_all)
    G_all = _dot(tri, g_hi) + _dot(tri, g_lo)
    G_last = G_all[C - 1:C, :]
    eG_all = jnp.exp(G_all)
    eGd_all = jnp.exp(G_last - G_all)
    eGl_all = jnp.exp(G_last)
    G_rows = jnp.concatenate([G_all, jnp.zeros((LANES - C, LANES), F32)], axis=0).T
    eye = jnp.where(r_io == c_io, 1.0, 0.0).astype(F32)

    outs = []
    for h in range(DN_HEADS):
        q = act[:, h * hd:(h + 1) * hd]
        k = act[:, DN_WIDTH + h * hd:DN_WIDTH + (h + 1) * hd]
        v = act[:, 2 * DN_WIDTH + h * hd:2 * DN_WIDTH + (h + 1) * hd]
        q = q * lax.rsqrt(jnp.sum(q * q, -1, keepdims=True) + NORM_EPS) * (hd ** -0.5)
        k = k * lax.rsqrt(jnp.sum(k * k, -1, keepdims=True) + NORM_EPS)
        beta = beta_all[:, h:h + 1]
        gl = DN_HEADS + h
        Gc = G_all[:, gl:gl + 1]
        Gr = G_rows[gl:gl + 1, 0:C]
        decay = jnp.where(incl, jnp.exp(jnp.where(incl, Gc - Gr, 0.0)), 0.0)
        kb = k * beta
        k16 = k.astype(BF16)
        M = jnp.where(strict, _dot_nt(kb.astype(BF16), k16) * decay, 0.0)
        P = eye - M
        Mp = M
        for _ in range(5):
            Mp = _mm3(Mp, Mp)
            P = P + _mm3(P, Mp)
        rhs = jnp.concatenate([v * beta, kb * eG_all[:, gl:gl + 1]], axis=1)
        sol = _mm3(P, rhs)
        u, w = sol[:, :hd], sol[:, hd:]
        attn = _dot_nt(q.astype(BF16), k16) * decay
        qd = q * eG_all[:, gl:gl + 1]
        kd = k * eGd_all[:, gl:gl + 1]
        S = st_ref[h]
        S16 = S.astype(BF16)
        wq = _dot(jnp.concatenate([w, qd], axis=0).astype(BF16), S16)
        v_new = u - wq[:C]
        v16 = v_new.astype(BF16)
        o = wq[C:] + _dot(attn.astype(BF16), v16)
        st_ref[h] = S * eGl_all[:, gl:gl + 1] + _dot(kd.T.astype(BF16), v16)
        o = o * lax.rsqrt(jnp.mean(o * o, -1, keepdims=True) + NORM_EPS) * ng_ref[...]
        outs.append(o * _silu(z_ref[:, h * hd:(h + 1) * hd]))
    o_ref[...] = jnp.concatenate(outs, axis=1).astype(BF16)


def _deltanet(qkv, z, ba, conv_w, alog_l, dtb_l, norm_g, B, S):
    N = S // CHUNK
    row = lambda w: pl.BlockSpec((CHUNK, w), lambda b, n: (b * N + n, 0))
    const = lambda s: pl.BlockSpec(s, lambda b, n: (0, 0))
    return pl.pallas_call(
        _deltanet_kernel,
        out_shape=jax.ShapeDtypeStruct((B * S, DN_WIDTH), BF16),
        grid=(B, N),
        in_specs=[row(3 * DN_WIDTH), row(DN_WIDTH), row(LANES), const((CONV_WIDTH, 3 * DN_WIDTH)),
                  const((1, LANES)), const((1, LANES)), const((1, DN_HEAD_DIM))],
        out_specs=row(DN_WIDTH),
        scratch_shapes=[pltpu.VMEM((CHUNK + SUBLANES, 3 * DN_WIDTH), F32),
                        pltpu.VMEM((DN_HEADS, DN_HEAD_DIM, DN_HEAD_DIM), F32)],
        compiler_params=_params(("parallel", "arbitrary")),
        name="gated_deltanet",
    )(qkv, z, ba, conv_w, alog_l, dtb_l, norm_g)


SB_BLOCK = 128


def _stick_kernel(q_ref, k_ref, v_ref, o_ref):
    i = pl.program_id(2)
    tq = q_ref.shape[0]
    tk = SB_BLOCK
    lane = lax.broadcasted_iota(I32, (tq, LANES), 1)
    r_io = lax.broadcasted_iota(I32, (tq, tk), 0)
    c_io = lax.broadcasted_iota(I32, (tq, tk), 1)
    causal = c_io < r_io
    u_r = lax.broadcasted_iota(I32, (2 * tk, tk), 0)
    u_c = lax.broadcasted_iota(I32, (2 * tk, tk), 1)
    later = jnp.where((u_r % tk) > u_c, 1.0, 0.0).astype(BF16)
    q = q_ref[...] * (SB_HEAD_DIM ** -0.5)
    zero = jnp.zeros_like(q)
    qs = [jnp.where((lane // SB_HEAD_DIM) == hh, q, zero) for hh in range(2)]

    def block(j, carry, diagonal):
        off = pl.multiple_of(j * tk, tk)
        kb = k_ref[pl.ds(off, tk), :]
        vb = v_ref[pl.ds(off, tk), :]
        new = []
        for hh in range(2):
            rest, acc = carry[2 * hh], carry[2 * hh + 1]
            z = _dot_nt(qs[hh], kb)
            log_beta = jnp.minimum(z, 0.0) - jnp.log1p(jnp.exp(-jnp.abs(z)))
            log_rest = log_beta - z
            if diagonal:
                log_rest = jnp.where(causal, log_rest, 0.0)
            hi, lo = _split_bf16(log_rest)
            after = _dot(jnp.concatenate([hi, lo], axis=1), later)
            a = jnp.exp(log_beta + after + rest)
            if diagonal:
                a = jnp.where(causal, a, 0.0)
            acc = acc + _dot(a.astype(BF16), vb)
            rest = rest + (after[:, 0:1] + log_rest[:, 0:1])
            new += [rest, acc]
        return tuple(new)

    init = (jnp.zeros((tq, 1), F32), jnp.zeros((tq, LANES), F32)) * 2
    carry = block(i, init, True)
    carry = lax.fori_loop(0, i, lambda t, c: block(i - 1 - t, c, False), carry)
    o_ref[...] = jnp.where(lane < SB_HEAD_DIM, carry[1], carry[3]).astype(BF16)


def _stick_breaking(sq, sk, sv, B, S):
    nq = S // SB_BLOCK
    pairs = SB_WIDTH // LANES
    q_spec = pl.BlockSpec((SB_BLOCK, LANES), lambda b, p, i: (b * nq + i, p))
    kv_spec = pl.BlockSpec((S, LANES), lambda b, p, i: (b, p))
    return pl.pallas_call(
        _stick_kernel,
        out_shape=jax.ShapeDtypeStruct((B * S, SB_WIDTH), BF16),
        grid=(B, pairs, nq),
        in_specs=[q_spec, kv_spec, kv_spec],
        out_specs=q_spec,
        compiler_params=_params(("parallel", "parallel", "arbitrary")),
        name="stick_breaking_attention",
    )(sq, sk, sv)


def _layer_norm(v, g, b):
    mu = jnp.mean(v, -1, keepdims=True)
    c = v - mu
    var = jnp.mean(c * c, -1, keepdims=True)
    return c * lax.rsqrt(var + LN_EPS) * g + b


def _silu(v):
    return v * jax.nn.sigmoid(v)


def _merge_kernel(oa_ref, ob_ref, ga_ref, gb_ref, x_ref, wpa_ref, wpb_ref, wout_ref, wr_ref, wsu_ref, wsd_ref,
                  g1_ref, b1_ref, h_ref, hs_ref, sc_ref):
    ya = _dot(oa_ref[...], wpa_ref[...])
    yb = _dot(ob_ref[...], wpb_ref[...])
    merged = jax.nn.sigmoid(ga_ref[...]) * ya + jax.nn.sigmoid(gb_ref[...]) * yb
    pre = DEEPNORM_ALPHA * x_ref[...] + _dot(merged.astype(BF16), wout_ref[...])
    h = _layer_norm(pre, g1_ref[...], b1_ref[...])
    h_ref[...] = h
    hb = h.astype(BF16)
    sc_ref[...] = jax.nn.sigmoid(_dot(hb, wr_ref[...]))
    up = _dot(hb, wsu_ref[...])
    act = _silu(up[:, :SHARED_FF]) * up[:, SHARED_FF:]
    hs_ref[...] = DEEPNORM_ALPHA * h + _dot(act.astype(BF16), wsd_ref[...])


def _merge(oa, ob, ga, gb, xf, wpa, wpb, wout, wr, wsu, wsd, g1, b1, tm=256):
    T = xf.shape[0]
    row = lambda w: pl.BlockSpec((tm, w), lambda i: (i, 0))
    return pl.pallas_call(
        _merge_kernel,
        out_shape=[jax.ShapeDtypeStruct((T, D_MODEL), F32), jax.ShapeDtypeStruct((T, D_MODEL), F32),
                   jax.ShapeDtypeStruct((T, N_EXPERTS), F32)],
        grid=(T // tm,),
        in_specs=[row(DN_WIDTH), row(SB_WIDTH), row(D_MODEL), row(D_MODEL), row(D_MODEL),
                  _const_spec(wpa.shape), _const_spec(wpb.shape), _const_spec(wout.shape), _const_spec(wr.shape),
                  _const_spec(wsu.shape), _const_spec(wsd.shape), _const_spec(g1.shape), _const_spec(b1.shape)],
        out_specs=[row(D_MODEL), row(D_MODEL), row(N_EXPERTS)],
        compiler_params=_params(("parallel",)),
        name="merge_norm_router_shared",
    )(oa, ob, ga, gb, xf, wpa, wpb, wout, wr, wsu, wsd, g1, b1)


def _route_kernel(sc_ref, bias_ref, idx_ref, gate_ref, rank_ref, cnt_ref, run_ref):
    i = pl.program_id(0)
    tm = sc_ref.shape[0]

    @pl.when(i == 0)
    def _():
        run_ref[...] = jnp.zeros_like(run_ref)

    scores = sc_ref[...]
    sel = scores + bias_ref[...]
    lane = lax.broadcasted_iota(I32, (tm, N_EXPERTS), 1)
    grp = lane // GROUP_SIZE
    neg = F32(-jnp.inf)

    def first_argmax(v):
        m = jnp.max(v, -1, keepdims=True)
        ix = jnp.min(jnp.where(v == m, lane, N_EXPERTS), -1, keepdims=True)
        return m, ix

    gscore = []
    for g in range(N_GROUPS):
        gv = jnp.where(grp == g, sel, neg)
        m1, i1 = first_argmax(gv)
        m2 = jnp.max(jnp.where(lane == i1, neg, gv), -1, keepdims=True)
        gscore.append(m1 + m2)
    emask = jnp.zeros((tm, N_EXPERTS), jnp.bool_)
    for g in range(N_GROUPS):
        beaten = jnp.zeros((tm, 1), I32)
        for o in range(N_GROUPS):
            if o == g:
                continue
            wins = (gscore[o] > gscore[g]) | ((gscore[o] == gscore[g]) & (o < g))
            beaten = beaten + wins.astype(I32)
        emask = emask | ((grp == g) & (beaten < TOP_GROUPS))
    masked = jnp.where(emask, sel, neg)

    picked = jnp.zeros((tm, N_EXPERTS), jnp.bool_)
    idx_cols, score_cols, hots = [], [], []
    for _ in range(TOP_K):
        _, ix = first_argmax(masked)
        hot = lane == ix
        idx_cols.append(ix)
        score_cols.append(jnp.sum(jnp.where(hot, scores, 0.0), -1, keepdims=True))
        hots.append(hot)
        masked = jnp.where(hot, neg, masked)
        picked = picked | hot
    chosen = jnp.where(picked, 1.0, 0.0).astype(BF16)
    r_io = lax.broadcasted_iota(I32, (tm, tm), 0)
    c_io = lax.broadcasted_iota(I32, (tm, tm), 1)
    strict_lower = jnp.where(c_io < r_io, 1.0, 0.0).astype(BF16)
    before = _dot(strict_lower, chosen) + run_ref[...]
    run_ref[...] = run_ref[...] + jnp.sum(chosen.astype(F32), 0, keepdims=True)
    cnt_ref[...] = run_ref[...]

    total = score_cols[0]
    for s in score_cols[1:]:
        total = total + s
    lane_o = lax.broadcasted_iota(I32, (tm, LANES), 1)
    idx_o = jnp.zeros((tm, LANES), I32)
    gate_o = jnp.zeros((tm, LANES), F32)
    rank_o = jnp.zeros((tm, LANES), I32)
    for k in range(TOP_K):
        rk = jnp.sum(jnp.where(hots[k], before, 0.0), -1, keepdims=True).astype(I32)
        idx_o = jnp.where(lane_o == k, idx_cols[k], idx_o)
        gate_o = jnp.where(lane_o == k, score_cols[k] / total * ROUTED_SCALE, gate_o)
        rank_o = jnp.where(lane_o == k, rk, rank_o)
    idx_ref[...] = idx_o
    gate_ref[...] = gate_o
    rank_ref[...] = rank_o


def _route(scores, bias, tm=256):
    T = scores.shape[0]
    row = lambda w: pl.BlockSpec((tm, w), lambda i: (i, 0))
    return pl.pallas_call(
        _route_kernel,
        out_shape=[jax.ShapeDtypeStruct((T, LANES), I32), jax.ShapeDtypeStruct((T, LANES), F32),
                   jax.ShapeDtypeStruct((T, LANES), I32), jax.ShapeDtypeStruct((1, N_EXPERTS), F32)],
        grid=(T // tm,),
        in_specs=[row(N_EXPERTS), pl.BlockSpec((1, N_EXPERTS), lambda i: (0, 0))],
        out_specs=[row(LANES), row(LANES), row(LANES), pl.BlockSpec((1, N_EXPERTS), lambda i: (0, 0))],
        scratch_shapes=[pltpu.VMEM((1, N_EXPERTS), F32)],
        compiler_params=_params(("arbitrary",)),
        name="route_topk",
    )(scores, bias)


DISPATCH_TOKENS = 128
_SLOTS = DISPATCH_TOKENS * TOP_K


def _dispatch_kernel(idx_ref, rank_ref, pstart_ref, h_hbm, xs_hbm, sem):
    i = pl.program_id(0)
    n = pl.num_programs(0)
    base = i * DISPATCH_TOKENS

    def row_copy(src_row, dst_row):
        return pltpu.make_async_copy(h_hbm.at[pl.ds(src_row, 1)], xs_hbm.at[pl.ds(dst_row, 1)], sem)

    def issue(t, carry):
        for k in range(TOP_K):
            j = t * TOP_K + k
            dst = pstart_ref[idx_ref[j]] + rank_ref[j]
            row_copy(base + t, dst).start()
        return carry

    lax.fori_loop(0, DISPATCH_TOKENS, issue, 0)

    def drain(_, carry):
        row_copy(0, 0).wait()
        return carry

    @pl.when(i > 0)
    def _():
        lax.fori_loop(0, _SLOTS, drain, 0)

    @pl.when(i == n - 1)
    def _():
        lax.fori_loop(0, _SLOTS, drain, 0)


def _dispatch(idx_flat, rank_flat, pstart, h, n_rows):
    T = h.shape[0]
    smem_blk = pl.BlockSpec((_SLOTS,), lambda i: (i,), memory_space=pltpu.SMEM)
    return pl.pallas_call(
        _dispatch_kernel,
        out_shape=jax.ShapeDtypeStruct((n_rows, D_MODEL), F32),
        grid=(T // DISPATCH_TOKENS,),
        in_specs=[smem_blk, smem_blk, pl.BlockSpec(memory_space=pltpu.SMEM), pl.BlockSpec(memory_space=pl.ANY)],
        out_specs=pl.BlockSpec(memory_space=pl.ANY),
        scratch_shapes=[pltpu.SemaphoreType.DMA(())],
        compiler_params=pltpu.CompilerParams(dimension_semantics=("arbitrary",), has_side_effects=True),
        name="dispatch_rows",
    )(idx_flat, rank_flat, pstart, h)


def _expert_kernel(be_ref, xs_ref, wu_ref, wd_ref, ys_ref):
    xb = xs_ref[...].astype(BF16)
    up = _dot(xb, wu_ref[0].astype(BF16))
    act = _silu(up[:, :EXPERT_FF]) * up[:, EXPERT_FF:]
    ys_ref[...] = _dot(act.astype(BF16), wd_ref[0].astype(BF16))


def _experts(block_e, xs, w_up, w_down):
    n_blocks = xs.shape[0] // EXPERT_BLOCK
    grid_spec = pltpu.PrefetchScalarGridSpec(
        num_scalar_prefetch=1,
        grid=(n_blocks,),
        in_specs=[pl.BlockSpec((EXPERT_BLOCK, D_MODEL), lambda b, be: (b, 0)),
                  pl.BlockSpec((1, D_MODEL, 2 * EXPERT_FF), lambda b, be: (be[b], 0, 0)),
                  pl.BlockSpec((1, EXPERT_FF, D_MODEL), lambda b, be: (be[b], 0, 0))],
        out_specs=pl.BlockSpec((EXPERT_BLOCK, D_MODEL), lambda b, be: (b, 0)),
    )
    return pl.pallas_call(
        _expert_kernel,
        out_shape=jax.ShapeDtypeStruct(xs.shape, F32),
        grid_spec=grid_spec,
        compiler_params=_params(("arbitrary",)),
        name="routed_experts",
    )(block_e, xs, w_up, w_down)


def _combine_kernel(idx_ref, rank_ref, pstart_ref, gate_ref, hs_ref, g2_ref, b2_ref, ys_hbm, out_ref, ybuf, sem):
    def row_copy(src_row, k, t):
        return pltpu.make_async_copy(ys_hbm.at[pl.ds(src_row, 1)], ybuf.at[k, pl.ds(t, 1)], sem)

    def issue(t, carry):
        for k in range(TOP_K):
            j = t * TOP_K + k
            src = pstart_ref[idx_ref[j]] + rank_ref[j]
            row_copy(src, k, t).start()
        return carry

    lax.fori_loop(0, DISPATCH_TOKENS, issue, 0)

    def drain(_, carry):
        row_copy(0, 0, 0).wait()
        return carry

    lax.fori_loop(0, _SLOTS, drain, 0)

    gates = gate_ref[...]
    acc = hs_ref[...]
    for k in range(TOP_K):
        acc = acc + gates[:, k:k + 1] * ybuf[k]
    out_ref[...] = _layer_norm(acc, g2_ref[...], b2_ref[...])


def _combine(idx_flat, rank_flat, pstart, gates, hs, g2, b2, ys):
    T = hs.shape[0]
    tm = DISPATCH_TOKENS
    smem_blk = pl.BlockSpec((_SLOTS,), lambda i: (i,), memory_space=pltpu.SMEM)
    return pl.pallas_call(
        _combine_kernel,
        out_shape=jax.ShapeDtypeStruct((T, D_MODEL), F32),
        grid=(T // tm,),
        in_specs=[smem_blk, smem_blk, pl.BlockSpec(memory_space=pltpu.SMEM),
                  pl.BlockSpec((tm, LANES), lambda i: (i, 0)), pl.BlockSpec((tm, D_MODEL), lambda i: (i, 0)),
                  pl.BlockSpec((1, D_MODEL), lambda i: (0, 0)), pl.BlockSpec((1, D_MODEL), lambda i: (0, 0)),
                  pl.BlockSpec(memory_space=pl.ANY)],
        out_specs=pl.BlockSpec((tm, D_MODEL), lambda i: (i, 0)),
        scratch_shapes=[pltpu.VMEM((TOP_K, tm, D_MODEL), F32), pltpu.SemaphoreType.DMA(())],
        compiler_params=_params(("arbitrary",)),
        name="combine_norm",
    )(idx_flat, rank_flat, pstart, gates, hs, g2, b2, ys)


def _pack_w_in(w_in):
    c1 = 4 * DN_WIDTH
    c2 = c1 + 2 * DN_HEADS
    ba = jnp.pad(w_in[:, c1:c2], ((0, 0), (0, LANES - 2 * DN_HEADS)))
    return jnp.concatenate([w_in[:, :c1], ba, w_in[:, c2:]], axis=1).astype(BF16)


def _head_lanes(v):
    return jnp.pad(v.astype(F32), (DN_HEADS, LANES - 2 * DN_HEADS)).reshape(1, LANES)


def _layer(x, w_in, conv_w, dn_a_log, dn_dt_bias, dn_norm_g, w_proj_a, w_proj_b, w_out, ln1_g, ln1_b, w_router,
           router_bias, w_shared_up, w_shared_down, w_expert_up, w_expert_down, ln2_g, ln2_b):
    B, S, D = x.shape
    T = B * S
    xf = x.reshape(T, D)
    qkv, z, ba, sq, sk, sv, ga, gb = _input_projection(xf, _pack_w_in(w_in))

    oa = _deltanet(qkv, z, ba, conv_w, _head_lanes(dn_a_log), _head_lanes(dn_dt_bias),
                   dn_norm_g.reshape(1, DN_HEAD_DIM), B, S)
    ob = _stick_breaking(sq, sk, sv, B, S)

    h, hs, scores = _merge(oa, ob, ga, gb, xf, w_proj_a.astype(BF16), w_proj_b.astype(BF16), w_out.astype(BF16),
                           w_router.astype(BF16), w_shared_up.astype(BF16), w_shared_down.astype(BF16),
                           ln1_g.reshape(1, D), ln1_b.reshape(1, D))

    idx, gates, rank, counts = _route(scores, router_bias.reshape(1, N_EXPERTS))

    counts = counts.reshape(N_EXPERTS).astype(I32)
    padded = (counts + EXPERT_BLOCK - 1) // EXPERT_BLOCK * EXPERT_BLOCK
    pend = jnp.cumsum(padded)
    pstart = (pend - padded).astype(I32)
    n_blocks = -(-(T * TOP_K + N_EXPERTS * (EXPERT_BLOCK - 1)) // EXPERT_BLOCK)
    block_start = jnp.arange(n_blocks, dtype=I32) * EXPERT_BLOCK
    block_e = jnp.minimum(jnp.sum(pend[None, :] <= block_start[:, None], 1), N_EXPERTS - 1).astype(I32)

    idx_flat = idx[:, :TOP_K].reshape(T * TOP_K)
    rank_flat = rank[:, :TOP_K].reshape(T * TOP_K)
    xs = _dispatch(idx_flat, rank_flat, pstart, h, n_blocks * EXPERT_BLOCK)
    ys = _experts(block_e, xs, w_expert_up, w_expert_down)
    out = _combine(idx_flat, rank_flat, pstart, gates, hs, ln2_g.reshape(1, D), ln2_b.reshape(1, D), ys)
    return out.reshape(B, S, D)


def kernel(x, w_in, conv_w, dn_a_log, dn_dt_bias, dn_norm_g, w_proj_a, w_proj_b, w_out, ln1_g, ln1_b, w_router,
           router_bias, w_shared_up, w_shared_down, w_expert_up, w_expert_down, ln2_g, ln2_b):
    for layer in range(DEPTH):
        x = _layer(x, w_in[layer], conv_w[layer], dn_a_log[layer], dn_dt_bias[layer], dn_norm_g[layer],
                   w_proj_a[layer], w_proj_b[layer], w_out[layer], ln1_g[layer], ln1_b[layer], w_router[layer],
                   router_bias[layer], w_shared_up[layer], w_shared_down[layer], w_expert_up[layer],
                   w_expert_down[layer], ln2_g[layer], ln2_b[layer])
    return x
```

```python
import functools
import math

import jax
import jax.numpy as jnp
import numpy as np
from jax import lax
from jax.experimental import pallas as pl
from jax.experimental.pallas import tpu as pltpu

F32 = jnp.float32
BF16 = jnp.bfloat16
I32 = jnp.int32

D_MODEL = 1024
CHUNK = 64
DN_HEADS = 4
DN_HEAD_DIM = 128
DN_WIDTH = DN_HEADS * DN_HEAD_DIM
CONV_WIDTH = 4
SB_HEADS = 8
SB_HEAD_DIM = 64
SB_WIDTH = SB_HEADS * SB_HEAD_DIM
N_EXPERTS = 256
TOP_K = 8
N_GROUPS = 8
GROUP_SIZE = N_EXPERTS // N_GROUPS
TOP_GROUPS = 4
EXPERT_FF = 256
SHARED_FF = 256
ROUTED_SCALE = 2.5
EXPERT_BLOCK = 128
DEPTH = 1
DEEPNORM_ALPHA = (2 * DEPTH) ** 0.25
LN_EPS = 1e-5
NORM_EPS = 1e-6

LANES = 128
SUBLANES = 8
VMEM_LIMIT = 56 * 1024 * 1024

_C_QKV = (0, 3 * DN_WIDTH)
_C_Z = (_C_QKV[1], _C_QKV[1] + DN_WIDTH)
_C_BA = (_C_Z[1], _C_Z[1] + LANES)
_C_SQ = (_C_BA[1], _C_BA[1] + SB_WIDTH)
_C_SK = (_C_SQ[1], _C_SQ[1] + SB_WIDTH)
_C_SV = (_C_SK[1], _C_SK[1] + SB_WIDTH)
_C_GA = (_C_SV[1], _C_SV[1] + D_MODEL)
_C_GB = (_C_GA[1], _C_GA[1] + D_MODEL)
PROJ_PACKED = _C_GB[1]


def _params(sem, vmem=VMEM_LIMIT):
    return pltpu.CompilerParams(dimension_semantics=sem, vmem_limit_bytes=vmem)


def _const_spec(shape):
    return pl.BlockSpec(shape, lambda *_: (0,) * len(shape), pipeline_mode=pl.Buffered(1))


def _dot(a, b):
    return jnp.dot(a, b, preferred_element_type=F32)


def _dot_nt(a, b):
    return lax.dot_general(a, b, (((1,), (1,)), ((), ())), preferred_element_type=F32)


def _split_bf16(a):
    hi = a.astype(BF16)
    return hi, (a - hi.astype(F32)).astype(BF16)


def _proj_kernel(x_ref, w_ref, qkv_ref, z_ref, ba_ref, sq_ref, sk_ref, sv_ref, ga_ref, gb_ref):
    xb = x_ref[...].astype(BF16)

    def mm(c):
        return _dot(xb, w_ref[:, c[0]:c[1]])

    qkv_ref[...] = mm(_C_QKV)
    z_ref[...] = mm(_C_Z)
    ba_ref[...] = mm(_C_BA)
    sq_ref[...] = mm(_C_SQ).astype(BF16)
    sk_ref[...] = mm(_C_SK).astype(BF16)
    sv_ref[...] = mm(_C_SV).astype(BF16)
    ga_ref[...] = mm(_C_GA)
    gb_ref[...] = mm(_C_GB)


def _input_projection(xf, w_packed, tm=256):
    T = xf.shape[0]
    widths = [(3 * DN_WIDTH, F32), (DN_WIDTH, F32), (LANES, F32), (SB_WIDTH, BF16), (SB_WIDTH, BF16),
              (SB_WIDTH, BF16), (D_MODEL, F32), (D_MODEL, F32)]
    return pl.pallas_call(
        _proj_kernel,
        out_shape=[jax.ShapeDtypeStruct((T, w), dt) for w, dt in widths],
        grid=(T // tm,),
        in_specs=[pl.BlockSpec((tm, D_MODEL), lambda i: (i, 0)), _const_spec((D_MODEL, PROJ_PACKED))],
        out_specs=[pl.BlockSpec((tm, w), lambda i: (i, 0)) for w, _ in widths],
        compiler_params=_params(("parallel",)),
        name="input_projection",
    )(xf, w_packed)


def _mm3(a, b):
    ah, al = _split_bf16(a)
    bh, bl = _split_bf16(b)
    return _dot(ah, bh) + _dot(ah, bl) + _dot(al, bh)


DN_STEP_CHUNKS = 4


def _deltanet_kernel(qkv_ref, z_ref, ba_ref, cw_ref, alog_ref, dtb_ref, ng_ref, o_ref, win_ref, st_ref):
    n = pl.program_id(1)
    C = CHUNK
    R = DN_STEP_CHUNKS * C
    hd = DN_HEAD_DIM

    @pl.when(n == 0)
    def _():
        win_ref[0:SUBLANES, :] = jnp.zeros((SUBLANES, 3 * DN_WIDTH), F32)
        st_ref[...] = jnp.zeros_like(st_ref)

    win_ref[SUBLANES:SUBLANES + R, :] = qkv_ref[...]
    conv = win_ref[pl.ds(SUBLANES - (CONV_WIDTH - 1), R), :] * cw_ref[0:1, :]
    for tap in range(1, CONV_WIDTH):
        conv = conv + win_ref[pl.ds(SUBLANES - (CONV_WIDTH - 1) + tap, R), :] * cw_ref[tap:tap + 1, :]
    win_ref[0:SUBLANES, :] = win_ref[R:R + SUBLANES, :]
    act_all = _silu(conv)

    bt = ba_ref[...]
    beta_rows = jax.nn.sigmoid(bt)
    g_rows = -jnp.exp(alog_ref[...]) * jax.nn.softplus(bt + dtb_ref[...])
    r_io = lax.broadcasted_iota(I32, (C, C), 0)
    c_io = lax.broadcasted_iota(I32, (C, C), 1)
    incl = r_io >= c_io
    strict = r_io > c_io
    tri = jnp.where(incl, 1.0, 0.0).astype(BF16)
    eye = jnp.where(r_io == c_io, 1.0, 0.0).astype(F32)
    pairs = [(c, h) for c in range(DN_STEP_CHUNKS) for h in range(DN_HEADS)]
    G, eG, eGd, eGl, Gt = [], [], [], [], []
    for c in range(DN_STEP_CHUNKS):
        g_hi, g_lo = _split_bf16(g_rows[c * C:(c + 1) * C])
        G.append(_dot(tri, g_hi) + _dot(tri, g_lo))
    for c in range(DN_STEP_CHUNKS):
        G_last = G[c][C - 1:C, :]
        eG.append(jnp.exp(G[c]))
        eGd.append(jnp.exp(G_last - G[c]))
        eGl.append(jnp.exp(G_last))
        Gt.append(jnp.concatenate([G[c], jnp.zeros((LANES - C, LANES), F32)], axis=0).T)

    q, k, k16, kb, v, beta, decay = {}, {}, {}, {}, {}, {}, {}
    for c, h in pairs:
        act = act_all[c * C:(c + 1) * C]
        qq = act[:, h * hd:(h + 1) * hd]
        kk = act[:, DN_WIDTH + h * hd:DN_WIDTH + (h + 1) * hd]
        v[c, h] = act[:, 2 * DN_WIDTH + h * hd:2 * DN_WIDTH + (h + 1) * hd]
        q[c, h] = qq * lax.rsqrt(jnp.sum(qq * qq, -1, keepdims=True) + NORM_EPS) * (hd ** -0.5)
        k[c, h] = kk * lax.rsqrt(jnp.sum(kk * kk, -1, keepdims=True) + NORM_EPS)
        beta[c, h] = beta_rows[c * C:(c + 1) * C, h:h + 1]
        gl = DN_HEADS + h
        diff = G[c][:, gl:gl + 1] - Gt[c][gl:gl + 1, 0:C]
        decay[c, h] = jnp.where(incl, jnp.exp(jnp.where(incl, diff, 0.0)), 0.0)
        kb[c, h] = k[c, h] * beta[c, h]
        k16[c, h] = k[c, h].astype(BF16)
    M = {p: jnp.where(strict, _dot_nt(kb[p].astype(BF16), k16[p]) * decay[p], 0.0) for p in pairs}
    attn = {p: (_dot_nt(q[p].astype(BF16), k16[p]) * decay[p]).astype(BF16) for p in pairs}
    P = {p: eye - M[p] for p in pairs}
    Mp = M
    for _ in range(5):
        Mp = {p: _mm3(Mp[p], Mp[p]) for p in pairs}
        P = {p: P[p] + _mm3(P[p], Mp[p]) for p in pairs}
    sol = {}
    for c, h in pairs:
        gl = DN_HEADS + h
        rhs = jnp.concatenate([v[c, h] * beta[c, h], kb[c, h] * eG[c][:, gl:gl + 1]], axis=1)
        sol[c, h] = _mm3(P[c, h], rhs)

    heads = range(DN_HEADS)
    S = [st_ref[h] for h in heads]
    for c in range(DN_STEP_CHUNKS):
        wq, vn, o = [], [], []
        for h in heads:
            gl = DN_HEADS + h
            qd = q[c, h] * eG[c][:, gl:gl + 1]
            wq.append(_dot(jnp.concatenate([sol[c, h][:, hd:], qd], axis=0).astype(BF16), S[h].astype(BF16)))
        for h in heads:
            vn.append((sol[c, h][:, :hd] - wq[h][:C]).astype(BF16))
        for h in heads:
            o.append(wq[h][C:] + _dot(attn[c, h], vn[h]))
        for h in heads:
            gl = DN_HEADS + h
            kd = k[c, h] * eGd[c][:, gl:gl + 1]
            S[h] = S[h] * eGl[c][:, gl:gl + 1] + _dot(kd.T.astype(BF16), vn[h])
        outs = []
        for h in heads:
            on = o[h] * lax.rsqrt(jnp.mean(o[h] * o[h], -1, keepdims=True) + NORM_EPS) * ng_ref[...]
            outs.append(on * _silu(z_ref[c * C:(c + 1) * C, h * hd:(h + 1) * hd]))
        o_ref[c * C:(c + 1) * C, :] = jnp.concatenate(outs, axis=1).astype(BF16)
    for h in heads:
        st_ref[h] = S[h]


def _deltanet(qkv, z, ba, conv_w, alog_l, dtb_l, norm_g, B, S):
    rows = DN_STEP_CHUNKS * CHUNK
    N = S // rows
    row = lambda w: pl.BlockSpec((rows, w), lambda b, n: (b * N + n, 0))
    const = lambda s: pl.BlockSpec(s, lambda b, n: (0, 0))
    return pl.pallas_call(
        _deltanet_kernel,
        out_shape=jax.ShapeDtypeStruct((B * S, DN_WIDTH), BF16),
        grid=(B, N),
        in_specs=[row(3 * DN_WIDTH), row(DN_WIDTH), row(LANES), const((CONV_WIDTH, 3 * DN_WIDTH)),
                  const((1, LANES)), const((1, LANES)), const((1, DN_HEAD_DIM))],
        out_specs=row(DN_WIDTH),
        scratch_shapes=[pltpu.VMEM((rows + SUBLANES, 3 * DN_WIDTH), F32),
                        pltpu.VMEM((DN_HEADS, DN_HEAD_DIM, DN_HEAD_DIM), F32)],
        compiler_params=_params(("parallel", "arbitrary")),
        name="gated_deltanet",
    )(qkv, z, ba, conv_w, alog_l, dtb_l, norm_g)


SB_TQ = 512
SB_TK = 256


def _stick_kernel(q_ref, k_ref, v_ref, o_ref):
    i = pl.program_id(2)
    tq, tk = SB_TQ, SB_TK
    ratio = tq // tk
    lane = lax.broadcasted_iota(I32, (tq, LANES), 1)
    r_io = lax.broadcasted_iota(I32, (tq, tk), 0)
    c_io = lax.broadcasted_iota(I32, (tq, tk), 1)
    u_r = lax.broadcasted_iota(I32, (2 * tk, tk), 0)
    u_c = lax.broadcasted_iota(I32, (2 * tk, tk), 1)
    later = jnp.where((u_r % tk) > u_c, 1.0, 0.0).astype(BF16)
    q = q_ref[...] * (SB_HEAD_DIM ** -0.5)
    zero = jnp.zeros_like(q)
    qs = [jnp.where((lane // SB_HEAD_DIM) == hh, q, zero) for hh in range(2)]

    def block(j, carry, masked):
        off = pl.multiple_of(j * tk, tk)
        kb = k_ref[pl.ds(off, tk), :]
        vb = v_ref[pl.ds(off, tk), :]
        if masked:
            causal = (c_io + j * tk) < (r_io + i * tq)
        zs = [_dot_nt(qs[hh], kb) for hh in range(2)]
        log_beta, log_rest, after = [], [], []
        for hh in range(2):
            lb = jnp.minimum(zs[hh], 0.0) - jnp.log1p(jnp.exp(-jnp.abs(zs[hh])))
            lr = lb - zs[hh]
            if masked:
                lr = jnp.where(causal, lr, 0.0)
            log_beta.append(lb)
            log_rest.append(lr)
        for hh in range(2):
            hi, lo = _split_bf16(log_rest[hh])
            after.append(_dot(jnp.concatenate([hi, lo], axis=1), later))
        new = []
        for hh in range(2):
            rest, acc = carry[2 * hh], carry[2 * hh + 1]
            a = jnp.exp(log_beta[hh] + after[hh] + rest)
            if masked:
                a = jnp.where(causal, a, 0.0)
            acc = acc + _dot(a.astype(BF16), vb)
            rest = rest + (after[hh][:, 0:1] + log_rest[hh][:, 0:1])
            new += [rest, acc]
        return tuple(new)

    carry = (jnp.zeros((tq, 1), F32), jnp.zeros((tq, LANES), F32)) * 2
    for d in range(ratio - 1, -1, -1):
        carry = block(i * ratio + d, carry, True)
    n_full = i * ratio
    carry = lax.fori_loop(0, n_full, lambda t, c: block(n_full - 1 - t, c, False), carry)
    o_ref[...] = jnp.where(lane < SB_HEAD_DIM, carry[1], carry[3]).astype(BF16)


def _stick_breaking(sq, sk, sv, B, S):
    nq = S // SB_TQ
    pairs = SB_WIDTH // LANES
    q_spec = pl.BlockSpec((SB_TQ, LANES), lambda b, p, i: (b * nq + i, p))
    kv_spec = pl.BlockSpec((S, LANES), lambda b, p, i: (b, p))
    return pl.pallas_call(
        _stick_kernel,
        out_shape=jax.ShapeDtypeStruct((B * S, SB_WIDTH), BF16),
        grid=(B, pairs, nq),
        in_specs=[q_spec, kv_spec, kv_spec],
        out_specs=q_spec,
        compiler_params=_params(("parallel", "parallel", "arbitrary")),
        name="stick_breaking_attention",
    )(sq, sk, sv)


def _layer_norm(v, g, b):
    mu = jnp.mean(v, -1, keepdims=True)
    c = v - mu
    var = jnp.mean(c * c, -1, keepdims=True)
    return c * lax.rsqrt(var + LN_EPS) * g + b


def _silu(v):
    return v * jax.nn.sigmoid(v)


def _merge_kernel(oa_ref, ob_ref, ga_ref, gb_ref, x_ref, wpa_ref, wpb_ref, wout_ref, wr_ref, wsu_ref, wsd_ref,
                  g1_ref, b1_ref, h_ref, hs_ref, sc_ref):
    ya = _dot(oa_ref[...], wpa_ref[...])
    yb = _dot(ob_ref[...], wpb_ref[...])
    merged = jax.nn.sigmoid(ga_ref[...]) * ya + jax.nn.sigmoid(gb_ref[...]) * yb
    pre = DEEPNORM_ALPHA * x_ref[...] + _dot(merged.astype(BF16), wout_ref[...])
    h = _layer_norm(pre, g1_ref[...], b1_ref[...])
    h_ref[...] = h
    hb = h.astype(BF16)
    sc_ref[...] = jax.nn.sigmoid(_dot(hb, wr_ref[...]))
    up = _dot(hb, wsu_ref[...])
    act = _silu(up[:, :SHARED_FF]) * up[:, SHARED_FF:]
    hs_ref[...] = DEEPNORM_ALPHA * h + _dot(act.astype(BF16), wsd_ref[...])


def _merge(oa, ob, ga, gb, xf, wpa, wpb, wout, wr, wsu, wsd, g1, b1, tm=256):
    T = xf.shape[0]
    row = lambda w: pl.BlockSpec((tm, w), lambda i: (i, 0))
    return pl.pallas_call(
        _merge_kernel,
        out_shape=[jax.ShapeDtypeStruct((T, D_MODEL), F32), jax.ShapeDtypeStruct((T, D_MODEL), F32),
                   jax.ShapeDtypeStruct((T, N_EXPERTS), F32)],
        grid=(T // tm,),
        in_specs=[row(DN_WIDTH), row(SB_WIDTH), row(D_MODEL), row(D_MODEL), row(D_MODEL),
                  _const_spec(wpa.shape), _const_spec(wpb.shape), _const_spec(wout.shape), _const_spec(wr.shape),
                  _const_spec(wsu.shape), _const_spec(wsd.shape), _const_spec(g1.shape), _const_spec(b1.shape)],
        out_specs=[row(D_MODEL), row(D_MODEL), row(N_EXPERTS)],
        compiler_params=_params(("parallel",)),
        name="merge_norm_router_shared",
    )(oa, ob, ga, gb, xf, wpa, wpb, wout, wr, wsu, wsd, g1, b1)


def _route_kernel(sc_ref, bias_ref, idx_ref, gate_ref, rank_ref, cnt_ref, run_ref):
    i = pl.program_id(0)
    tm = sc_ref.shape[0]

    @pl.when(i == 0)
    def _():
        run_ref[...] = jnp.zeros_like(run_ref)

    scores = sc_ref[...]
    sel = scores + bias_ref[...]
    lane = lax.broadcasted_iota(I32, (tm, N_EXPERTS), 1)
    grp = lane // GROUP_SIZE
    neg = F32(-jnp.inf)

    def first_argmax(v):
        m = jnp.max(v, -1, keepdims=True)
        ix = jnp.min(jnp.where(v == m, lane, N_EXPERTS), -1, keepdims=True)
        return m, ix

    gscore = []
    for g in range(N_GROUPS):
        gv = jnp.where(grp == g, sel, neg)
        m1, i1 = first_argmax(gv)
        m2 = jnp.max(jnp.where(lane == i1, neg, gv), -1, keepdims=True)
        gscore.append(m1 + m2)
    emask = jnp.zeros((tm, N_EXPERTS), jnp.bool_)
    for g in range(N_GROUPS):
        beaten = jnp.zeros((tm, 1), I32)
        for o in range(N_GROUPS):
            if o == g:
                continue
            wins = (gscore[o] > gscore[g]) | ((gscore[o] == gscore[g]) & (o < g))
            beaten = beaten + wins.astype(I32)
        emask = emask | ((grp == g) & (beaten < TOP_GROUPS))
    masked = jnp.where(emask, sel, neg)

    picked = jnp.zeros((tm, N_EXPERTS), jnp.bool_)
    idx_cols, score_cols, hots = [], [], []
    for _ in range(TOP_K):
        _, ix = first_argmax(masked)
        hot = lane == ix
        idx_cols.append(ix)
        score_cols.append(jnp.sum(jnp.where(hot, scores, 0.0), -1, keepdims=True))
        hots.append(hot)
        masked = jnp.where(hot, neg, masked)
        picked = picked | hot
    chosen = jnp.where(picked, 1.0, 0.0).astype(BF16)
    r_io = lax.broadcasted_iota(I32, (tm, tm), 0)
    c_io = lax.broadcasted_iota(I32, (tm, tm), 1)
    strict_lower = jnp.where(c_io < r_io, 1.0, 0.0).astype(BF16)
    before = _dot(strict_lower, chosen) + run_ref[...]
    run_ref[...] = run_ref[...] + jnp.sum(chosen.astype(F32), 0, keepdims=True)
    cnt_ref[...] = run_ref[...]

    total = score_cols[0]
    for s in score_cols[1:]:
        total = total + s
    lane_o = lax.broadcasted_iota(I32, (tm, LANES), 1)
    idx_o = jnp.zeros((tm, LANES), I32)
    gate_o = jnp.zeros((tm, LANES), F32)
    rank_o = jnp.zeros((tm, LANES), I32)
    for k in range(TOP_K):
        rk = jnp.sum(jnp.where(hots[k], before, 0.0), -1, keepdims=True).astype(I32)
        idx_o = jnp.where(lane_o == k, idx_cols[k], idx_o)
        gate_o = jnp.where(lane_o == k, score_cols[k] / total * ROUTED_SCALE, gate_o)
        rank_o = jnp.where(lane_o == k, rk, rank_o)
    idx_ref[...] = idx_o
    gate_ref[...] = gate_o
    rank_ref[...] = rank_o


def _route(scores, bias, tm=256):
    T = scores.shape[0]
    row = lambda w: pl.BlockSpec((tm, w), lambda i: (i, 0))
    return pl.pallas_call(
        _route_kernel,
        out_shape=[jax.ShapeDtypeStruct((T, LANES), I32), jax.ShapeDtypeStruct((T, LANES), F32),
                   jax.ShapeDtypeStruct((T, LANES), I32), jax.ShapeDtypeStruct((1, N_EXPERTS), F32)],
        grid=(T // tm,),
        in_specs=[row(N_EXPERTS), pl.BlockSpec((1, N_EXPERTS), lambda i: (0, 0))],
        out_specs=[row(LANES), row(LANES), row(LANES), pl.BlockSpec((1, N_EXPERTS), lambda i: (0, 0))],
        scratch_shapes=[pltpu.VMEM((1, N_EXPERTS), F32)],
        compiler_params=_params(("arbitrary",)),
        name="route_topk",
    )(scores, bias)


DISPATCH_TOKENS = 256
COMBINE_TOKENS = 128
EXPERT_ROWS = 256


def _dispatch_kernel(idx_ref, rank_ref, pstart_ref, h_ref, xs_hbm, sem):
    def issue(t, carry):
        for k in range(TOP_K):
            j = t * TOP_K + k
            dst = pstart_ref[idx_ref[j]] + rank_ref[j]
            pltpu.make_async_copy(h_ref.at[pl.ds(t, 1)], xs_hbm.at[pl.ds(dst, 1)], sem).start(priority=k % 2)
        return carry

    lax.fori_loop(0, DISPATCH_TOKENS, issue, 0)
    for _ in range(TOP_K):
        pltpu.make_async_copy(h_ref, xs_hbm.at[pl.ds(0, DISPATCH_TOKENS)], sem).wait()


def _dispatch(idx_flat, rank_flat, pstart, h, n_rows):
    T = h.shape[0]
    tm = DISPATCH_TOKENS
    smem_blk = pl.BlockSpec((tm * TOP_K,), lambda i: (i,), memory_space=pltpu.SMEM)
    return pl.pallas_call(
        _dispatch_kernel,
        out_shape=jax.ShapeDtypeStruct((n_rows, D_MODEL), F32),
        grid=(T // tm,),
        in_specs=[smem_blk, smem_blk, pl.BlockSpec(memory_space=pltpu.SMEM),
                  pl.BlockSpec((tm, D_MODEL), lambda i: (i, 0))],
        out_specs=pl.BlockSpec(memory_space=pl.ANY),
        scratch_shapes=[pltpu.SemaphoreType.DMA(())],
        compiler_params=pltpu.CompilerParams(dimension_semantics=("arbitrary",), has_side_effects=True,
                                             vmem_limit_bytes=VMEM_LIMIT),
        name="dispatch_rows",
    )(idx_flat, rank_flat, pstart, h)


def _expert_kernel(blk_ref, be_ref, nu_ref, xs_ref, wu_ref, wd_ref, ys_ref, wu16_ref, wd16_ref):
    b = pl.program_id(0)
    prev_e = be_ref[jnp.maximum(b - 1, 0)]

    @pl.when((b == 0) | (be_ref[b] != prev_e))
    def _():
        wu16_ref[...] = wu_ref[0].astype(BF16)
        wd16_ref[...] = wd_ref[0].astype(BF16)

    @pl.when(b < nu_ref[0])
    def _():
        xb = xs_ref[...].astype(BF16)
        up = _dot(xb, wu16_ref[...])
        act = _silu(up[:, :EXPERT_FF]) * up[:, EXPERT_FF:]
        ys_ref[...] = _dot(act.astype(BF16), wd16_ref[...])


def _experts(block_idx, block_e, n_used, xs, w_up, w_down):
    n_blocks = xs.shape[0] // EXPERT_ROWS
    grid_spec = pltpu.PrefetchScalarGridSpec(
        num_scalar_prefetch=3,
        grid=(n_blocks,),
        in_specs=[pl.BlockSpec((EXPERT_ROWS, D_MODEL), lambda b, bi, be, nu: (bi[b], 0)),
                  pl.BlockSpec((1, D_MODEL, 2 * EXPERT_FF), lambda b, bi, be, nu: (be[b], 0, 0)),
                  pl.BlockSpec((1, EXPERT_FF, D_MODEL), lambda b, bi, be, nu: (be[b], 0, 0))],
        out_specs=pl.BlockSpec((EXPERT_ROWS, D_MODEL), lambda b, bi, be, nu: (bi[b], 0)),
        scratch_shapes=[pltpu.VMEM((D_MODEL, 2 * EXPERT_FF), BF16), pltpu.VMEM((EXPERT_FF, D_MODEL), BF16)],
    )
    return pl.pallas_call(
        _expert_kernel,
        out_shape=jax.ShapeDtypeStruct(xs.shape, F32),
        grid_spec=grid_spec,
        compiler_params=_params(("arbitrary",)),
        name="routed_experts",
    )(block_idx, block_e, n_used, xs, w_up, w_down)


def _combine_kernel(idx_ref, rank_ref, idxn_ref, rankn_ref, pstart_ref, gate_ref, hs_ref, g2_ref, b2_ref, ys_hbm,
                    out_ref, ybuf, sems):
    i = pl.program_id(0)
    n = pl.num_programs(0)
    slot = i % 2

    def gather(ir, rr, s):
        def issue(t, carry):
            for k in range(TOP_K):
                j = t * TOP_K + k
                src = pstart_ref[ir[j]] + rr[j]
                pltpu.make_async_copy(ys_hbm.at[pl.ds(src, 1)], ybuf.at[s, k, pl.ds(t, 1)],
                                      sems.at[s]).start(priority=k % 2)
            return carry

        lax.fori_loop(0, COMBINE_TOKENS, issue, 0)

    @pl.when(i == 0)
    def _():
        gather(idx_ref, rank_ref, 0)

    @pl.when(i + 1 < n)
    def _():
        gather(idxn_ref, rankn_ref, 1 - slot)

    for k in range(TOP_K):
        pltpu.make_async_copy(ys_hbm.at[pl.ds(0, COMBINE_TOKENS)], ybuf.at[slot, k], sems.at[slot]).wait()

    gates = gate_ref[...]
    acc = hs_ref[...]
    for k in range(TOP_K):
        acc = acc + gates[:, k:k + 1] * ybuf[slot, k]
    out_ref[...] = _layer_norm(acc, g2_ref[...], b2_ref[...])


def _combine(idx_flat, rank_flat, pstart, gates, hs, g2, b2, ys):
    T = hs.shape[0]
    tm = COMBINE_TOKENS
    n = T // tm
    cur = pl.BlockSpec((tm * TOP_K,), lambda i: (i,), memory_space=pltpu.SMEM)
    nxt = pl.BlockSpec((tm * TOP_K,), lambda i: (jnp.minimum(i + 1, n - 1),), memory_space=pltpu.SMEM)
    return pl.pallas_call(
        _combine_kernel,
        out_shape=jax.ShapeDtypeStruct((T, D_MODEL), F32),
        grid=(n,),
        in_specs=[cur, cur, nxt, nxt, pl.BlockSpec(memory_space=pltpu.SMEM),
                  pl.BlockSpec((tm, LANES), lambda i: (i, 0)), pl.BlockSpec((tm, D_MODEL), lambda i: (i, 0)),
                  pl.BlockSpec((1, D_MODEL), lambda i: (0, 0)), pl.BlockSpec((1, D_MODEL), lambda i: (0, 0)),
                  pl.BlockSpec(memory_space=pl.ANY)],
        out_specs=pl.BlockSpec((tm, D_MODEL), lambda i: (i, 0)),
        scratch_shapes=[pltpu.VMEM((2, TOP_K, tm, D_MODEL), F32), pltpu.SemaphoreType.DMA((2,))],
        compiler_params=_params(("arbitrary",)),
        name="combine_norm",
    )(idx_flat, rank_flat, idx_flat, rank_flat, pstart, gates, hs, g2, b2, ys)


def _pack_w_in(w_in):
    c1 = 4 * DN_WIDTH
    c2 = c1 + 2 * DN_HEADS
    ba = jnp.pad(w_in[:, c1:c2], ((0, 0), (0, LANES - 2 * DN_HEADS)))
    return jnp.concatenate([w_in[:, :c1], ba, w_in[:, c2:]], axis=1).astype(BF16)


def _head_lanes(v):
    return jnp.pad(v.astype(F32), (DN_HEADS, LANES - 2 * DN_HEADS)).reshape(1, LANES)


def _layer(x, w_in, conv_w, dn_a_log, dn_dt_bias, dn_norm_g, w_proj_a, w_proj_b, w_out, ln1_g, ln1_b, w_router,
           router_bias, w_shared_up, w_shared_down, w_expert_up, w_expert_down, ln2_g, ln2_b):
    B, S, D = x.shape
    T = B * S
    xf = x.reshape(T, D)
    qkv, z, ba, sq, sk, sv, ga, gb = _input_projection(xf, _pack_w_in(w_in))

    oa = _deltanet(qkv, z, ba, conv_w, _head_lanes(dn_a_log), _head_lanes(dn_dt_bias),
                   dn_norm_g.reshape(1, DN_HEAD_DIM), B, S)
    ob = _stick_breaking(sq, sk, sv, B, S)

    h, hs, scores = _merge(oa, ob, ga, gb, xf, w_proj_a.astype(BF16), w_proj_b.astype(BF16), w_out.astype(BF16),
                           w_router.astype(BF16), w_shared_up.astype(BF16), w_shared_down.astype(BF16),
                           ln1_g.reshape(1, D), ln1_b.reshape(1, D))

    idx, gates, rank, counts = _route(scores, router_bias.reshape(1, N_EXPERTS))

    counts = counts.reshape(N_EXPERTS).astype(I32)
    padded = (counts + EXPERT_ROWS - 1) // EXPERT_ROWS * EXPERT_ROWS
    pend = jnp.cumsum(padded)
    pstart = (pend - padded).astype(I32)
    n_blocks = -(-(T * TOP_K + N_EXPERTS * (EXPERT_ROWS - 1)) // EXPERT_ROWS)
    n_used = (pend[-1] // EXPERT_ROWS).astype(I32)
    block_idx = jnp.minimum(jnp.arange(n_blocks, dtype=I32), n_used - 1)
    block_e = jnp.minimum(jnp.sum(pend[None, :] <= (block_idx * EXPERT_ROWS)[:, None], 1), N_EXPERTS - 1).astype(I32)

    idx_flat = idx[:, :TOP_K].reshape(T * TOP_K)
    rank_flat = rank[:, :TOP_K].reshape(T * TOP_K)
    xs = _dispatch(idx_flat, rank_flat, pstart, h, n_blocks * EXPERT_ROWS)
    ys = _experts(block_idx, block_e, n_used.reshape(1), xs, w_expert_up, w_expert_down)
    out = _combine(idx_flat, rank_flat, pstart, gates, hs, ln2_g.reshape(1, D), ln2_b.reshape(1, D), ys)
    return out.reshape(B, S, D)


def kernel(x, w_in, conv_w, dn_a_log, dn_dt_bias, dn_norm_g, w_proj_a, w_proj_b, w_out, ln1_g, ln1_b, w_router,
           router_bias, w_shared_up, w_shared_down, w_expert_up, w_expert_down, ln2_g, ln2_b):
    for layer in range(DEPTH):
        x = _layer(x, w_in[layer], conv_w[layer], dn_a_log[layer], dn_dt_bias[layer], dn_norm_g[layer],
                   w_proj_a[layer], w_proj_b[layer], w_out[layer], ln1_g[layer], ln1_b[layer], w_router[layer],
                   router_bias[layer], w_shared_up[layer], w_shared_down[layer], w_expert_up[layer],
                   w_expert_down[layer], ln2_g[layer], ln2_b[layer])
    return x
```

```python
import functools
import math

import jax
import jax.numpy as jnp
import numpy as np
from jax import lax
from jax.experimental import pallas as pl
from jax.experimental.pallas import tpu as pltpu

F32 = jnp.float32
BF16 = jnp.bfloat16
I32 = jnp.int32

D_MODEL = 1024
CHUNK = 64
DN_HEADS = 4
DN_HEAD_DIM = 128
DN_WIDTH = DN_HEADS * DN_HEAD_DIM
CONV_WIDTH = 4
SB_HEADS = 8
SB_HEAD_DIM = 64
SB_WIDTH = SB_HEADS * SB_HEAD_DIM
N_EXPERTS = 256
TOP_K = 8
N_GROUPS = 8
GROUP_SIZE = N_EXPERTS // N_GROUPS
TOP_GROUPS = 4
EXPERT_FF = 256
SHARED_FF = 256
ROUTED_SCALE = 2.5
EXPERT_BLOCK = 128
DEPTH = 1
DEEPNORM_ALPHA = (2 * DEPTH) ** 0.25
LN_EPS = 1e-5
NORM_EPS = 1e-6

LANES = 128
SUBLANES = 8
VMEM_LIMIT = 56 * 1024 * 1024

_C_QKV = (0, 3 * DN_WIDTH)
_C_Z = (_C_QKV[1], _C_QKV[1] + DN_WIDTH)
_C_BA = (_C_Z[1], _C_Z[1] + LANES)
_C_SQ = (_C_BA[1], _C_BA[1] + SB_WIDTH)
_C_SK = (_C_SQ[1], _C_SQ[1] + SB_WIDTH)
_C_SV = (_C_SK[1], _C_SK[1] + SB_WIDTH)
_C_GA = (_C_SV[1], _C_SV[1] + D_MODEL)
_C_GB = (_C_GA[1], _C_GA[1] + D_MODEL)
PROJ_PACKED = _C_GB[1]


def _params(sem, vmem=VMEM_LIMIT):
    return pltpu.CompilerParams(dimension_semantics=sem, vmem_limit_bytes=vmem)


def _const_spec(shape):
    return pl.BlockSpec(shape, lambda *_: (0,) * len(shape), pipeline_mode=pl.Buffered(1))


def _dot(a, b):
    return jnp.dot(a, b, preferred_element_type=F32)


def _dot_nt(a, b):
    return lax.dot_general(a, b, (((1,), (1,)), ((), ())), preferred_element_type=F32)


def _split_bf16(a):
    hi = a.astype(BF16)
    return hi, (a - hi.astype(F32)).astype(BF16)


U32 = jnp.uint32
HALF_MODEL = D_MODEL // 2
ROW_TILE = HALF_MODEL // LANES
HIGH_HALF_MASK = 0xFFFF0000


def _store_row_tiles(ref, mat):
    n = mat.shape[0]
    lo = lax.bitcast_convert_type(mat[:, :HALF_MODEL].astype(BF16).astype(F32), U32)
    hi = lax.bitcast_convert_type(mat[:, HALF_MODEL:].astype(BF16).astype(F32), U32)
    words = (lo >> 16) | hi
    for s in range(ROW_TILE):
        ref[pl.ds(s, n, stride=ROW_TILE), :] = words[:, s * LANES:(s + 1) * LANES]


def _load_row_tiles(ref, n, lead=()):
    words = [ref[lead + (pl.ds(s, n, stride=ROW_TILE), slice(None))] for s in range(ROW_TILE)]
    lo = [lax.bitcast_convert_type(w << 16, F32) for w in words]
    hi = [lax.bitcast_convert_type(w & U32(HIGH_HALF_MASK), F32) for w in words]
    return jnp.concatenate(lo + hi, axis=1)


def _proj_kernel(x_ref, w_ref, qkv_ref, z_ref, ba_ref, sq_ref, sk_ref, sv_ref, ga_ref, gb_ref):
    xb = x_ref[...].astype(BF16)

    def mm(c):
        return _dot(xb, w_ref[:, c[0]:c[1]])

    qkv_ref[...] = mm(_C_QKV)
    z_ref[...] = mm(_C_Z)
    ba_ref[...] = mm(_C_BA)
    sq_ref[...] = mm(_C_SQ).astype(BF16)
    sk_ref[...] = mm(_C_SK).astype(BF16)
    sv_ref[...] = mm(_C_SV).astype(BF16)
    ga_ref[...] = mm(_C_GA)
    gb_ref[...] = mm(_C_GB)


def _input_projection(xf, w_packed, tm=256):
    T = xf.shape[0]
    widths = [(3 * DN_WIDTH, F32), (DN_WIDTH, F32), (LANES, F32), (SB_WIDTH, BF16), (SB_WIDTH, BF16),
              (SB_WIDTH, BF16), (D_MODEL, F32), (D_MODEL, F32)]
    return pl.pallas_call(
        _proj_kernel,
        out_shape=[jax.ShapeDtypeStruct((T, w), dt) for w, dt in widths],
        grid=(T // tm,),
        in_specs=[pl.BlockSpec((tm, D_MODEL), lambda i: (i, 0)), _const_spec((D_MODEL, PROJ_PACKED))],
        out_specs=[pl.BlockSpec((tm, w), lambda i: (i, 0)) for w, _ in widths],
        compiler_params=_params(("parallel",)),
        name="input_projection",
    )(xf, w_packed)


def _mm3(a, b):
    ah, al = _split_bf16(a)
    bh, bl = _split_bf16(b)
    return _dot(ah, bh) + _dot(ah, bl) + _dot(al, bh)


DN_STEP_CHUNKS = 4


def _deltanet_kernel(qkv_ref, z_ref, ba_ref, cw_ref, alog_ref, dtb_ref, ng_ref, o_ref, win_ref, st_ref):
    n = pl.program_id(1)
    C = CHUNK
    R = DN_STEP_CHUNKS * C
    hd = DN_HEAD_DIM

    @pl.when(n == 0)
    def _():
        win_ref[0:SUBLANES, :] = jnp.zeros((SUBLANES, 3 * DN_WIDTH), F32)
        st_ref[...] = jnp.zeros_like(st_ref)

    win_ref[SUBLANES:SUBLANES + R, :] = qkv_ref[...]
    conv = win_ref[pl.ds(SUBLANES - (CONV_WIDTH - 1), R), :] * cw_ref[0:1, :]
    for tap in range(1, CONV_WIDTH):
        conv = conv + win_ref[pl.ds(SUBLANES - (CONV_WIDTH - 1) + tap, R), :] * cw_ref[tap:tap + 1, :]
    win_ref[0:SUBLANES, :] = win_ref[R:R + SUBLANES, :]
    act_all = _silu(conv)

    bt = ba_ref[...]
    beta_rows = jax.nn.sigmoid(bt)
    g_rows = -jnp.exp(alog_ref[...]) * jax.nn.softplus(bt + dtb_ref[...])
    r_io = lax.broadcasted_iota(I32, (C, C), 0)
    c_io = lax.broadcasted_iota(I32, (C, C), 1)
    incl = r_io >= c_io
    strict = r_io > c_io
    tri = jnp.where(incl, 1.0, 0.0).astype(BF16)
    eye = jnp.where(r_io == c_io, 1.0, 0.0).astype(F32)
    pairs = [(c, h) for c in range(DN_STEP_CHUNKS) for h in range(DN_HEADS)]
    G, eG, eGd, eGl, Gt = [], [], [], [], []
    for c in range(DN_STEP_CHUNKS):
        g_hi, g_lo = _split_bf16(g_rows[c * C:(c + 1) * C])
        G.append(_dot(tri, g_hi) + _dot(tri, g_lo))
    for c in range(DN_STEP_CHUNKS):
        G_last = G[c][C - 1:C, :]
        eG.append(jnp.exp(G[c]))
        eGd.append(jnp.exp(G_last - G[c]))
        eGl.append(jnp.exp(G_last))
        Gt.append(jnp.concatenate([G[c], jnp.zeros((LANES - C, LANES), F32)], axis=0).T)

    q, k, k16, kb, v, beta, decay = {}, {}, {}, {}, {}, {}, {}
    for c, h in pairs:
        act = act_all[c * C:(c + 1) * C]
        qq = act[:, h * hd:(h + 1) * hd]
        kk = act[:, DN_WIDTH + h * hd:DN_WIDTH + (h + 1) * hd]
        v[c, h] = act[:, 2 * DN_WIDTH + h * hd:2 * DN_WIDTH + (h + 1) * hd]
        q[c, h] = qq * lax.rsqrt(jnp.sum(qq * qq, -1, keepdims=True) + NORM_EPS) * (hd ** -0.5)
        k[c, h] = kk * lax.rsqrt(jnp.sum(kk * kk, -1, keepdims=True) + NORM_EPS)
        beta[c, h] = beta_rows[c * C:(c + 1) * C, h:h + 1]
        gl = DN_HEADS + h
        diff = G[c][:, gl:gl + 1] - Gt[c][gl:gl + 1, 0:C]
        decay[c, h] = jnp.where(incl, jnp.exp(jnp.where(incl, diff, 0.0)), 0.0)
        kb[c, h] = k[c, h] * beta[c, h]
        k16[c, h] = k[c, h].astype(BF16)
    M = {p: jnp.where(strict, _dot_nt(kb[p].astype(BF16), k16[p]) * decay[p], 0.0) for p in pairs}
    attn = {p: (_dot_nt(q[p].astype(BF16), k16[p]) * decay[p]).astype(BF16) for p in pairs}
    P = {p: eye - M[p] for p in pairs}
    Mp = M
    for _ in range(5):
        Mp = {p: _mm3(Mp[p], Mp[p]) for p in pairs}
        P = {p: P[p] + _mm3(P[p], Mp[p]) for p in pairs}
    sol = {}
    for c, h in pairs:
        gl = DN_HEADS + h
        rhs = jnp.concatenate([v[c, h] * beta[c, h], kb[c, h] * eG[c][:, gl:gl + 1]], axis=1)
        sol[c, h] = _mm3(P[c, h], rhs)

    heads = range(DN_HEADS)
    S = [st_ref[h] for h in heads]
    for c in range(DN_STEP_CHUNKS):
        wq, vn, o = [], [], []
        for h in heads:
            gl = DN_HEADS + h
            qd = q[c, h] * eG[c][:, gl:gl + 1]
            wq.append(_dot(jnp.concatenate([sol[c, h][:, hd:], qd], axis=0).astype(BF16), S[h].astype(BF16)))
        for h in heads:
            vn.append((sol[c, h][:, :hd] - wq[h][:C]).astype(BF16))
        for h in heads:
            o.append(wq[h][C:] + _dot(attn[c, h], vn[h]))
        for h in heads:
            gl = DN_HEADS + h
            kd = k[c, h] * eGd[c][:, gl:gl + 1]
            S[h] = S[h] * eGl[c][:, gl:gl + 1] + _dot(kd.T.astype(BF16), vn[h])
        outs = []
        for h in heads:
            on = o[h] * lax.rsqrt(jnp.mean(o[h] * o[h], -1, keepdims=True) + NORM_EPS) * ng_ref[...]
            outs.append(on * _silu(z_ref[c * C:(c + 1) * C, h * hd:(h + 1) * hd]))
        o_ref[c * C:(c + 1) * C, :] = jnp.concatenate(outs, axis=1).astype(BF16)
    for h in heads:
        st_ref[h] = S[h]


def _deltanet(qkv, z, ba, conv_w, alog_l, dtb_l, norm_g, B, S):
    rows = DN_STEP_CHUNKS * CHUNK
    N = S // rows
    row = lambda w: pl.BlockSpec((rows, w), lambda b, n: (b * N + n, 0))
    const = lambda s: pl.BlockSpec(s, lambda b, n: (0, 0))
    return pl.pallas_call(
        _deltanet_kernel,
        out_shape=jax.ShapeDtypeStruct((B * S, DN_WIDTH), BF16),
        grid=(B, N),
        in_specs=[row(3 * DN_WIDTH), row(DN_WIDTH), row(LANES), const((CONV_WIDTH, 3 * DN_WIDTH)),
                  const((1, LANES)), const((1, LANES)), const((1, DN_HEAD_DIM))],
        out_specs=row(DN_WIDTH),
        scratch_shapes=[pltpu.VMEM((rows + SUBLANES, 3 * DN_WIDTH), F32),
                        pltpu.VMEM((DN_HEADS, DN_HEAD_DIM, DN_HEAD_DIM), F32)],
        compiler_params=_params(("parallel", "arbitrary")),
        name="gated_deltanet",
    )(qkv, z, ba, conv_w, alog_l, dtb_l, norm_g)


SB_TQ = 512
SB_TK = 256


def _stick_kernel(q_ref, k_ref, v_ref, o_ref):
    i = pl.program_id(2)
    tq, tk = SB_TQ, SB_TK
    ratio = tq // tk
    lane = lax.broadcasted_iota(I32, (tq, LANES), 1)
    r_io = lax.broadcasted_iota(I32, (tq, tk), 0)
    c_io = lax.broadcasted_iota(I32, (tq, tk), 1)
    u_r = lax.broadcasted_iota(I32, (2 * tk, tk), 0)
    u_c = lax.broadcasted_iota(I32, (2 * tk, tk), 1)
    later = jnp.where((u_r % tk) > u_c, 1.0, 0.0).astype(BF16)
    q = q_ref[...] * (SB_HEAD_DIM ** -0.5)
    zero = jnp.zeros_like(q)
    qs = [jnp.where((lane // SB_HEAD_DIM) == hh, q, zero) for hh in range(2)]

    def block(j, carry, masked):
        off = pl.multiple_of(j * tk, tk)
        kb = k_ref[pl.ds(off, tk), :]
        vb = v_ref[pl.ds(off, tk), :]
        if masked:
            causal = (c_io + j * tk) < (r_io + i * tq)
        zs = [_dot_nt(qs[hh], kb) for hh in range(2)]
        log_beta, log_rest, after = [], [], []
        for hh in range(2):
            lb = jnp.minimum(zs[hh], 0.0) - jnp.log1p(jnp.exp(-jnp.abs(zs[hh])))
            lr = lb - zs[hh]
            if masked:
                lr = jnp.where(causal, lr, 0.0)
            log_beta.append(lb)
            log_rest.append(lr)
        for hh in range(2):
            hi, lo = _split_bf16(log_rest[hh])
            after.append(_dot(jnp.concatenate([hi, lo], axis=1), later))
        new = []
        for hh in range(2):
            rest, acc = carry[2 * hh], carry[2 * hh + 1]
            a = jnp.exp(log_beta[hh] + after[hh] + rest)
            if masked:
                a = jnp.where(causal, a, 0.0)
            acc = acc + _dot(a.astype(BF16), vb)
            rest = rest + (after[hh][:, 0:1] + log_rest[hh][:, 0:1])
            new += [rest, acc]
        return tuple(new)

    carry = (jnp.zeros((tq, 1), F32), jnp.zeros((tq, LANES), F32)) * 2
    for d in range(ratio - 1, -1, -1):
        carry = block(i * ratio + d, carry, True)
    n_full = i * ratio
    carry = lax.fori_loop(0, n_full, lambda t, c: block(n_full - 1 - t, c, False), carry)
    o_ref[...] = jnp.where(lane < SB_HEAD_DIM, carry[1], carry[3]).astype(BF16)


def _stick_breaking(sq, sk, sv, B, S):
    nq = S // SB_TQ
    pairs = SB_WIDTH // LANES
    q_spec = pl.BlockSpec((SB_TQ, LANES), lambda b, p, i: (b * nq + i, p))
    kv_spec = pl.BlockSpec((S, LANES), lambda b, p, i: (b, p))
    return pl.pallas_call(
        _stick_kernel,
        out_shape=jax.ShapeDtypeStruct((B * S, SB_WIDTH), BF16),
        grid=(B, pairs, nq),
        in_specs=[q_spec, kv_spec, kv_spec],
        out_specs=q_spec,
        compiler_params=_params(("parallel", "parallel", "arbitrary")),
        name="stick_breaking_attention",
    )(sq, sk, sv)


def _layer_norm(v, g, b):
    mu = jnp.mean(v, -1, keepdims=True)
    c = v - mu
    var = jnp.mean(c * c, -1, keepdims=True)
    return c * lax.rsqrt(var + LN_EPS) * g + b


def _silu(v):
    return v * jax.nn.sigmoid(v)


def _merge_kernel(oa_ref, ob_ref, ga_ref, gb_ref, x_ref, wpa_ref, wpb_ref, wout_ref, wr_ref, wsu_ref, wsd_ref,
                  g1_ref, b1_ref, h_ref, hs_ref, sc_ref):
    ya = _dot(oa_ref[...], wpa_ref[...])
    yb = _dot(ob_ref[...], wpb_ref[...])
    merged = jax.nn.sigmoid(ga_ref[...]) * ya + jax.nn.sigmoid(gb_ref[...]) * yb
    pre = DEEPNORM_ALPHA * x_ref[...] + _dot(merged.astype(BF16), wout_ref[...])
    h = _layer_norm(pre, g1_ref[...], b1_ref[...])
    _store_row_tiles(h_ref, h)
    hb = h.astype(BF16)
    sc_ref[...] = jax.nn.sigmoid(_dot(hb, wr_ref[...]))
    up = _dot(hb, wsu_ref[...])
    act = _silu(up[:, :SHARED_FF]) * up[:, SHARED_FF:]
    hs_ref[...] = DEEPNORM_ALPHA * h + _dot(act.astype(BF16), wsd_ref[...])


def _merge(oa, ob, ga, gb, xf, wpa, wpb, wout, wr, wsu, wsd, g1, b1, tm=256):
    T = xf.shape[0]
    row = lambda w: pl.BlockSpec((tm, w), lambda i: (i, 0))
    return pl.pallas_call(
        _merge_kernel,
        out_shape=[jax.ShapeDtypeStruct((T * ROW_TILE, LANES), U32), jax.ShapeDtypeStruct((T, D_MODEL), F32),
                   jax.ShapeDtypeStruct((T, N_EXPERTS), F32)],
        grid=(T // tm,),
        in_specs=[row(DN_WIDTH), row(SB_WIDTH), row(D_MODEL), row(D_MODEL), row(D_MODEL),
                  _const_spec(wpa.shape), _const_spec(wpb.shape), _const_spec(wout.shape), _const_spec(wr.shape),
                  _const_spec(wsu.shape), _const_spec(wsd.shape), _const_spec(g1.shape), _const_spec(b1.shape)],
        out_specs=[pl.BlockSpec((tm * ROW_TILE, LANES), lambda i: (i, 0)), row(D_MODEL), row(N_EXPERTS)],
        compiler_params=_params(("parallel",)),
        name="merge_norm_router_shared",
    )(oa, ob, ga, gb, xf, wpa, wpb, wout, wr, wsu, wsd, g1, b1)


def _route_kernel(sc_ref, bias_ref, idx_ref, gate_ref, rank_ref, cnt_ref, run_ref):
    i = pl.program_id(0)
    tm = sc_ref.shape[0]

    @pl.when(i == 0)
    def _():
        run_ref[...] = jnp.zeros_like(run_ref)

    scores = sc_ref[...]
    sel = scores + bias_ref[...]
    lane_i = lax.broadcasted_iota(I32, (tm, N_EXPERTS), 1)
    grp = lane_i // GROUP_SIZE
    lane = lane_i.astype(F32)
    neg = F32(-jnp.inf)

    def first_argmax(v):
        m = jnp.max(v, -1, keepdims=True)
        ix = jnp.min(jnp.where(v == m, lane, float(N_EXPERTS)), -1, keepdims=True)
        return m, ix

    gscore = []
    for g in range(N_GROUPS):
        gv = jnp.where(grp == g, sel, neg)
        m1, i1 = first_argmax(gv)
        m2 = jnp.max(jnp.where(lane == i1, neg, gv), -1, keepdims=True)
        gscore.append(m1 + m2)
    emask = jnp.zeros((tm, N_EXPERTS), jnp.bool_)
    for g in range(N_GROUPS):
        beaten = jnp.zeros((tm, 1), I32)
        for o in range(N_GROUPS):
            if o == g:
                continue
            wins = (gscore[o] > gscore[g]) | ((gscore[o] == gscore[g]) & (o < g))
            beaten = beaten + wins.astype(I32)
        emask = emask | ((grp == g) & (beaten < TOP_GROUPS))
    masked = jnp.where(emask, sel, neg)

    picked = jnp.zeros((tm, N_EXPERTS), jnp.bool_)
    idx_cols, score_cols, hots = [], [], []
    for _ in range(TOP_K):
        _, ix = first_argmax(masked)
        hot = lane == ix
        idx_cols.append(ix)
        score_cols.append(jnp.sum(jnp.where(hot, scores, 0.0), -1, keepdims=True))
        hots.append(hot)
        masked = jnp.where(hot, neg, masked)
        picked = picked | hot
    chosen = jnp.where(picked, 1.0, 0.0).astype(BF16)
    r_io = lax.broadcasted_iota(I32, (tm, tm), 0)
    c_io = lax.broadcasted_iota(I32, (tm, tm), 1)
    strict_lower = jnp.where(c_io < r_io, 1.0, 0.0).astype(BF16)
    before = _dot(strict_lower, chosen) + run_ref[...]
    run_ref[...] = run_ref[...] + jnp.sum(chosen.astype(F32), 0, keepdims=True)
    cnt_ref[...] = run_ref[...]

    total = score_cols[0]
    for s in score_cols[1:]:
        total = total + s
    lane_o = lax.broadcasted_iota(I32, (tm, LANES), 1)
    idx_o = jnp.zeros((tm, LANES), F32)
    gate_o = jnp.zeros((tm, LANES), F32)
    rank_o = jnp.zeros((tm, LANES), F32)
    for k in range(TOP_K):
        rk = jnp.sum(jnp.where(hots[k], before, 0.0), -1, keepdims=True)
        idx_o = jnp.where(lane_o == k, idx_cols[k], idx_o)
        gate_o = jnp.where(lane_o == k, score_cols[k] / total * ROUTED_SCALE, gate_o)
        rank_o = jnp.where(lane_o == k, rk, rank_o)
    idx_ref[...] = idx_o.astype(I32)
    gate_ref[...] = gate_o
    rank_ref[...] = rank_o.astype(I32)


def _route(scores, bias, tm=256):
    T = scores.shape[0]
    row = lambda w: pl.BlockSpec((tm, w), lambda i: (i, 0))
    return pl.pallas_call(
        _route_kernel,
        out_shape=[jax.ShapeDtypeStruct((T, LANES), I32), jax.ShapeDtypeStruct((T, LANES), F32),
                   jax.ShapeDtypeStruct((T, LANES), I32), jax.ShapeDtypeStruct((1, N_EXPERTS), F32)],
        grid=(T // tm,),
        in_specs=[row(N_EXPERTS), pl.BlockSpec((1, N_EXPERTS), lambda i: (0, 0))],
        out_specs=[row(LANES), row(LANES), row(LANES), pl.BlockSpec((1, N_EXPERTS), lambda i: (0, 0))],
        scratch_shapes=[pltpu.VMEM((1, N_EXPERTS), F32)],
        compiler_params=_params(("arbitrary",)),
        name="route_topk",
    )(scores, bias)


DISPATCH_TOKENS = 256
COMBINE_TOKENS = 128
EXPERT_ROWS = 256


def _dispatch_kernel(dest_ref, h_ref, xs_hbm, sem):
    def issue(t, carry):
        src = pl.multiple_of(t * ROW_TILE, ROW_TILE)
        for k in range(TOP_K):
            dst = pl.multiple_of(dest_ref[t * TOP_K + k] * ROW_TILE, ROW_TILE)
            pltpu.make_async_copy(h_ref.at[pl.ds(src, ROW_TILE)], xs_hbm.at[pl.ds(dst, ROW_TILE)],
                                  sem).start(priority=k % 2)
        return carry

    lax.fori_loop(0, DISPATCH_TOKENS, issue, 0)
    for _ in range(TOP_K):
        pltpu.make_async_copy(h_ref, xs_hbm.at[pl.ds(0, DISPATCH_TOKENS * ROW_TILE)], sem).wait()


def _dispatch(dest_flat, h_rt, n_rows):
    T = h_rt.shape[0] // ROW_TILE
    tm = DISPATCH_TOKENS
    return pl.pallas_call(
        _dispatch_kernel,
        out_shape=jax.ShapeDtypeStruct((n_rows * ROW_TILE, LANES), U32),
        grid=(T // tm,),
        in_specs=[pl.BlockSpec((tm * TOP_K,), lambda i: (i,), memory_space=pltpu.SMEM),
                  pl.BlockSpec((tm * ROW_TILE, LANES), lambda i: (i, 0))],
        out_specs=pl.BlockSpec(memory_space=pl.ANY),
        scratch_shapes=[pltpu.SemaphoreType.DMA(())],
        compiler_params=pltpu.CompilerParams(dimension_semantics=("arbitrary",), has_side_effects=True,
                                             vmem_limit_bytes=VMEM_LIMIT),
        name="dispatch_rows",
    )(dest_flat, h_rt)


def _expert_kernel(blk_ref, be_ref, nu_ref, xs_ref, wu_ref, wd_ref, ys_ref, wu16_ref, wd16_ref):
    b = pl.program_id(0)
    prev_e = be_ref[jnp.maximum(b - 1, 0)]

    @pl.when((b == 0) | (be_ref[b] != prev_e))
    def _():
        wu16_ref[...] = wu_ref[0].astype(BF16)
        wd16_ref[...] = wd_ref[0].astype(BF16)

    @pl.when(b < nu_ref[0])
    def _():
        xb = _load_row_tiles(xs_ref, EXPERT_ROWS).astype(BF16)
        up = _dot(xb, wu16_ref[...])
        act = _silu(up[:, :EXPERT_FF]) * up[:, EXPERT_FF:]
        _store_row_tiles(ys_ref, _dot(act.astype(BF16), wd16_ref[...]))


def _experts(block_idx, block_e, n_used, xs, w_up, w_down):
    n_blocks = xs.shape[0] // (EXPERT_ROWS * ROW_TILE)
    grid_spec = pltpu.PrefetchScalarGridSpec(
        num_scalar_prefetch=3,
        grid=(n_blocks,),
        in_specs=[pl.BlockSpec((EXPERT_ROWS * ROW_TILE, LANES), lambda b, bi, be, nu: (bi[b], 0)),
                  pl.BlockSpec((1, D_MODEL, 2 * EXPERT_FF), lambda b, bi, be, nu: (be[b], 0, 0)),
                  pl.BlockSpec((1, EXPERT_FF, D_MODEL), lambda b, bi, be, nu: (be[b], 0, 0))],
        out_specs=pl.BlockSpec((EXPERT_ROWS * ROW_TILE, LANES), lambda b, bi, be, nu: (bi[b], 0)),
        scratch_shapes=[pltpu.VMEM((D_MODEL, 2 * EXPERT_FF), BF16), pltpu.VMEM((EXPERT_FF, D_MODEL), BF16)],
    )
    return pl.pallas_call(
        _expert_kernel,
        out_shape=jax.ShapeDtypeStruct(xs.shape, U32),
        grid_spec=grid_spec,
        compiler_params=_params(("arbitrary",)),
        name="routed_experts",
    )(block_idx, block_e, n_used, xs, w_up, w_down)


def _combine_kernel(dest_ref, destn_ref, gate_ref, hs_ref, g2_ref, b2_ref, ys_hbm, out_ref, ybuf, sems):
    i = pl.program_id(0)
    n = pl.num_programs(0)
    slot = i % 2

    def gather(dr, s):
        def issue(t, carry):
            dst = pl.multiple_of(t * ROW_TILE, ROW_TILE)
            for k in range(TOP_K):
                src = pl.multiple_of(dr[t * TOP_K + k] * ROW_TILE, ROW_TILE)
                pltpu.make_async_copy(ys_hbm.at[pl.ds(src, ROW_TILE)], ybuf.at[s, k, pl.ds(dst, ROW_TILE)],
                                      sems.at[s]).start(priority=k % 2)
            return carry

        lax.fori_loop(0, COMBINE_TOKENS, issue, 0)

    @pl.when(i == 0)
    def _():
        gather(dest_ref, 0)

    @pl.when(i + 1 < n)
    def _():
        gather(destn_ref, 1 - slot)

    for k in range(TOP_K):
        pltpu.make_async_copy(ys_hbm.at[pl.ds(0, COMBINE_TOKENS * ROW_TILE)], ybuf.at[slot, k],
                              sems.at[slot]).wait()

    gates = gate_ref[...]
    acc = hs_ref[...]
    for k in range(TOP_K):
        acc = acc + gates[:, k:k + 1] * _load_row_tiles(ybuf, COMBINE_TOKENS, (slot, k))
    out_ref[...] = _layer_norm(acc, g2_ref[...], b2_ref[...])


def _combine(dest_flat, gates, hs, g2, b2, ys):
    T = hs.shape[0]
    tm = COMBINE_TOKENS
    n = T // tm
    cur = pl.BlockSpec((tm * TOP_K,), lambda i: (i,), memory_space=pltpu.SMEM)
    nxt = pl.BlockSpec((tm * TOP_K,), lambda i: (jnp.minimum(i + 1, n - 1),), memory_space=pltpu.SMEM)
    return pl.pallas_call(
        _combine_kernel,
        out_shape=jax.ShapeDtypeStruct((T, D_MODEL), F32),
        grid=(n,),
        in_specs=[cur, nxt, pl.BlockSpec((tm, LANES), lambda i: (i, 0)), pl.BlockSpec((tm, D_MODEL), lambda i: (i, 0)),
                  pl.BlockSpec((1, D_MODEL), lambda i: (0, 0)), pl.BlockSpec((1, D_MODEL), lambda i: (0, 0)),
                  pl.BlockSpec(memory_space=pl.ANY)],
        out_specs=pl.BlockSpec((tm, D_MODEL), lambda i: (i, 0)),
        scratch_shapes=[pltpu.VMEM((2, TOP_K, tm * ROW_TILE, LANES), U32), pltpu.SemaphoreType.DMA((2,))],
        compiler_params=_params(("arbitrary",)),
        name="combine_norm",
    )(dest_flat, dest_flat, gates, hs, g2, b2, ys)


def _pack_w_in(w_in):
    c1 = 4 * DN_WIDTH
    c2 = c1 + 2 * DN_HEADS
    ba = jnp.pad(w_in[:, c1:c2], ((0, 0), (0, LANES - 2 * DN_HEADS)))
    return jnp.concatenate([w_in[:, :c1], ba, w_in[:, c2:]], axis=1).astype(BF16)


def _head_lanes(v):
    return jnp.pad(v.astype(F32), (DN_HEADS, LANES - 2 * DN_HEADS)).reshape(1, LANES)


def _layer(x, w_in, conv_w, dn_a_log, dn_dt_bias, dn_norm_g, w_proj_a, w_proj_b, w_out, ln1_g, ln1_b, w_router,
           router_bias, w_shared_up, w_shared_down, w_expert_up, w_expert_down, ln2_g, ln2_b):
    B, S, D = x.shape
    T = B * S
    xf = x.reshape(T, D)
    qkv, z, ba, sq, sk, sv, ga, gb = _input_projection(xf, _pack_w_in(w_in))

    oa = _deltanet(qkv, z, ba, conv_w, _head_lanes(dn_a_log), _head_lanes(dn_dt_bias),
                   dn_norm_g.reshape(1, DN_HEAD_DIM), B, S)
    ob = _stick_breaking(sq, sk, sv, B, S)

    h, hs, scores = _merge(oa, ob, ga, gb, xf, w_proj_a.astype(BF16), w_proj_b.astype(BF16), w_out.astype(BF16),
                           w_router.astype(BF16), w_shared_up.astype(BF16), w_shared_down.astype(BF16),
                           ln1_g.reshape(1, D), ln1_b.reshape(1, D))

    idx, gates, rank, counts = _route(scores, router_bias.reshape(1, N_EXPERTS))

    counts = counts.reshape(N_EXPERTS).astype(I32)
    padded = (counts + EXPERT_ROWS - 1) // EXPERT_ROWS * EXPERT_ROWS
    pend = jnp.cumsum(padded)
    pstart = (pend - padded).astype(I32)
    n_blocks = -(-(T * TOP_K + N_EXPERTS * (EXPERT_ROWS - 1)) // EXPERT_ROWS)
    n_used = (pend[-1] // EXPERT_ROWS).astype(I32)
    block_idx = jnp.minimum(jnp.arange(n_blocks, dtype=I32), n_used - 1)
    block_e = jnp.minimum(jnp.sum(pend[None, :] <= (block_idx * EXPERT_ROWS)[:, None], 1), N_EXPERTS - 1).astype(I32)

    dest_flat = (pstart[idx[:, :TOP_K]] + rank[:, :TOP_K]).reshape(T * TOP_K)
    xs = _dispatch(dest_flat, h, n_blocks * EXPERT_ROWS)
    ys = _experts(block_idx, block_e, n_used.reshape(1), xs, w_expert_up, w_expert_down)
    out = _combine(dest_flat, gates, hs, ln2_g.reshape(1, D), ln2_b.reshape(1, D), ys)
    return out.reshape(B, S, D)


def kernel(x, w_in, conv_w, dn_a_log, dn_dt_bias, dn_norm_g, w_proj_a, w_proj_b, w_out, ln1_g, ln1_b, w_router,
           router_bias, w_shared_up, w_shared_down, w_expert_up, w_expert_down, ln2_g, ln2_b):
    for layer in range(DEPTH):
        x = _layer(x, w_in[layer], conv_w[layer], dn_a_log[layer], dn_dt_bias[layer], dn_norm_g[layer],
                   w_proj_a[layer], w_proj_b[layer], w_out[layer], ln1_g[layer], ln1_b[layer], w_router[layer],
                   router_bias[layer], w_shared_up[layer], w_shared_down[layer], w_expert_up[layer],
                   w_expert_down[layer], ln2_g[layer], ln2_b[layer])
    return x
```

```python
import functools
import math

import jax
import jax.numpy as jnp
import numpy as np
from jax import lax
from jax.experimental import pallas as pl
from jax.experimental.pallas import tpu as pltpu

F32 = jnp.float32
BF16 = jnp.bfloat16
I32 = jnp.int32

D_MODEL = 1024
CHUNK = 64
DN_HEADS = 4
DN_HEAD_DIM = 128
DN_WIDTH = DN_HEADS * DN_HEAD_DIM
CONV_WIDTH = 4
SB_HEADS = 8
SB_HEAD_DIM = 64
SB_WIDTH = SB_HEADS * SB_HEAD_DIM
N_EXPERTS = 256
TOP_K = 8
N_GROUPS = 8
GROUP_SIZE = N_EXPERTS // N_GROUPS
TOP_GROUPS = 4
EXPERT_FF = 256
SHARED_FF = 256
ROUTED_SCALE = 2.5
EXPERT_BLOCK = 128
DEPTH = 1
DEEPNORM_ALPHA = (2 * DEPTH) ** 0.25
LN_EPS = 1e-5
NORM_EPS = 1e-6

LANES = 128
SUBLANES = 8
VMEM_LIMIT = 56 * 1024 * 1024

_C_QKV = (0, 3 * DN_WIDTH)
_C_Z = (_C_QKV[1], _C_QKV[1] + DN_WIDTH)
_C_BA = (_C_Z[1], _C_Z[1] + LANES)
_C_SQ = (_C_BA[1], _C_BA[1] + SB_WIDTH)
_C_SK = (_C_SQ[1], _C_SQ[1] + SB_WIDTH)
_C_SV = (_C_SK[1], _C_SK[1] + SB_WIDTH)
_C_GA = (_C_SV[1], _C_SV[1] + D_MODEL)
_C_GB = (_C_GA[1], _C_GA[1] + D_MODEL)
PROJ_PACKED = _C_GB[1]


def _params(sem, vmem=VMEM_LIMIT):
    return pltpu.CompilerParams(dimension_semantics=sem, vmem_limit_bytes=vmem)


def _const_spec(shape):
    return pl.BlockSpec(shape, lambda *_: (0,) * len(shape), pipeline_mode=pl.Buffered(1))


def _dot(a, b):
    return jnp.dot(a, b, preferred_element_type=F32)


def _dot_nt(a, b):
    return lax.dot_general(a, b, (((1,), (1,)), ((), ())), preferred_element_type=F32)


def _split_bf16(a):
    hi = a.astype(BF16)
    return hi, (a - hi.astype(F32)).astype(BF16)


U32 = jnp.uint32
HALF_MODEL = D_MODEL // 2
ROW_TILE = HALF_MODEL // LANES
HIGH_HALF_MASK = 0xFFFF0000


def _store_row_tiles(ref, mat):
    n = mat.shape[0]
    lo = lax.bitcast_convert_type(mat[:, :HALF_MODEL].astype(BF16).astype(F32), U32)
    hi = lax.bitcast_convert_type(mat[:, HALF_MODEL:].astype(BF16).astype(F32), U32)
    words = (lo >> 16) | hi
    for s in range(ROW_TILE):
        ref[pl.ds(s, n, stride=ROW_TILE), :] = words[:, s * LANES:(s + 1) * LANES]


def _load_row_tiles(ref, n, lead=()):
    words = [ref[lead + (pl.ds(s, n, stride=ROW_TILE), slice(None))] for s in range(ROW_TILE)]
    lo = [lax.bitcast_convert_type(w << 16, F32) for w in words]
    hi = [lax.bitcast_convert_type(w & U32(HIGH_HALF_MASK), F32) for w in words]
    return jnp.concatenate(lo + hi, axis=1)


def _proj_kernel(x_ref, w_ref, qkv_ref, z_ref, ba_ref, sq_ref, sk_ref, sv_ref, ga_ref, gb_ref):
    xb = x_ref[...].astype(BF16)

    def mm(c):
        return _dot(xb, w_ref[:, c[0]:c[1]])

    qkv_ref[...] = mm(_C_QKV)
    z_ref[...] = mm(_C_Z)
    ba_ref[...] = mm(_C_BA)
    sq_ref[...] = mm(_C_SQ).astype(BF16)
    sk_ref[...] = mm(_C_SK).astype(BF16)
    sv_ref[...] = mm(_C_SV).astype(BF16)
    ga_ref[...] = mm(_C_GA)
    gb_ref[...] = mm(_C_GB)


def _input_projection(xf, w_packed, tm=256):
    T = xf.shape[0]
    widths = [(3 * DN_WIDTH, F32), (DN_WIDTH, F32), (LANES, F32), (SB_WIDTH, BF16), (SB_WIDTH, BF16),
              (SB_WIDTH, BF16), (D_MODEL, F32), (D_MODEL, F32)]
    return pl.pallas_call(
        _proj_kernel,
        out_shape=[jax.ShapeDtypeStruct((T, w), dt) for w, dt in widths],
        grid=(T // tm,),
        in_specs=[pl.BlockSpec((tm, D_MODEL), lambda i: (i, 0)), _const_spec((D_MODEL, PROJ_PACKED))],
        out_specs=[pl.BlockSpec((tm, w), lambda i: (i, 0)) for w, _ in widths],
        compiler_params=_params(("parallel",)),
        name="input_projection",
    )(xf, w_packed)


def _mm3(a, b):
    ah, al = _split_bf16(a)
    bh, bl = _split_bf16(b)
    return _dot(ah, bh) + _dot(ah, bl) + _dot(al, bh)


DN_STEP_CHUNKS = 4


def _deltanet_kernel(qkv_ref, z_ref, ba_ref, cw_ref, alog_ref, dtb_ref, ng_ref, o_ref, win_ref, st_ref):
    n = pl.program_id(1)
    C = CHUNK
    R = DN_STEP_CHUNKS * C
    hd = DN_HEAD_DIM

    @pl.when(n == 0)
    def _():
        win_ref[0:SUBLANES, :] = jnp.zeros((SUBLANES, 3 * DN_WIDTH), F32)
        st_ref[...] = jnp.zeros_like(st_ref)

    win_ref[SUBLANES:SUBLANES + R, :] = qkv_ref[...]
    conv = win_ref[pl.ds(SUBLANES - (CONV_WIDTH - 1), R), :] * cw_ref[0:1, :]
    for tap in range(1, CONV_WIDTH):
        conv = conv + win_ref[pl.ds(SUBLANES - (CONV_WIDTH - 1) + tap, R), :] * cw_ref[tap:tap + 1, :]
    win_ref[0:SUBLANES, :] = win_ref[R:R + SUBLANES, :]
    act_all = _silu(conv)

    bt = ba_ref[...]
    beta_rows = jax.nn.sigmoid(bt)
    g_rows = -jnp.exp(alog_ref[...]) * jax.nn.softplus(bt + dtb_ref[...])
    r_io = lax.broadcasted_iota(I32, (C, C), 0)
    c_io = lax.broadcasted_iota(I32, (C, C), 1)
    incl = r_io >= c_io
    strict = r_io > c_io
    tri = jnp.where(incl, 1.0, 0.0).astype(BF16)
    eye = jnp.where(r_io == c_io, 1.0, 0.0).astype(F32)
    pairs = [(c, h) for c in range(DN_STEP_CHUNKS) for h in range(DN_HEADS)]
    G, eG, eGd, eGl, Gt = [], [], [], [], []
    for c in range(DN_STEP_CHUNKS):
        g_hi, g_lo = _split_bf16(g_rows[c * C:(c + 1) * C])
        G.append(_dot(tri, g_hi) + _dot(tri, g_lo))
    for c in range(DN_STEP_CHUNKS):
        G_last = G[c][C - 1:C, :]
        eG.append(jnp.exp(G[c]))
        eGd.append(jnp.exp(G_last - G[c]))
        eGl.append(jnp.exp(G_last))
        Gt.append(jnp.concatenate([G[c], jnp.zeros((LANES - C, LANES), F32)], axis=0).T)

    q, k, k16, kb, v, beta, decay = {}, {}, {}, {}, {}, {}, {}
    for c, h in pairs:
        act = act_all[c * C:(c + 1) * C]
        qq = act[:, h * hd:(h + 1) * hd]
        kk = act[:, DN_WIDTH + h * hd:DN_WIDTH + (h + 1) * hd]
        v[c, h] = act[:, 2 * DN_WIDTH + h * hd:2 * DN_WIDTH + (h + 1) * hd]
        q[c, h] = qq * lax.rsqrt(jnp.sum(qq * qq, -1, keepdims=True) + NORM_EPS) * (hd ** -0.5)
        k[c, h] = kk * lax.rsqrt(jnp.sum(kk * kk, -1, keepdims=True) + NORM_EPS)
        beta[c, h] = beta_rows[c * C:(c + 1) * C, h:h + 1]
        gl = DN_HEADS + h
        diff = G[c][:, gl:gl + 1] - Gt[c][gl:gl + 1, 0:C]
        decay[c, h] = jnp.where(incl, jnp.exp(jnp.where(incl, diff, 0.0)), 0.0)
        kb[c, h] = k[c, h] * beta[c, h]
        k16[c, h] = k[c, h].astype(BF16)
    M = {p: jnp.where(strict, _dot_nt(kb[p].astype(BF16), k16[p]) * decay[p], 0.0) for p in pairs}
    attn = {p: (_dot_nt(q[p].astype(BF16), k16[p]) * decay[p]).astype(BF16) for p in pairs}
    P = {p: eye - M[p] for p in pairs}
    Mp = M
    for _ in range(5):
        Mp = {p: _mm3(Mp[p], Mp[p]) for p in pairs}
        P = {p: P[p] + _mm3(P[p], Mp[p]) for p in pairs}
    sol = {}
    for c, h in pairs:
        gl = DN_HEADS + h
        rhs = jnp.concatenate([v[c, h] * beta[c, h], kb[c, h] * eG[c][:, gl:gl + 1]], axis=1)
        sol[c, h] = _mm3(P[c, h], rhs)

    heads = range(DN_HEADS)
    S = [st_ref[h] for h in heads]
    for c in range(DN_STEP_CHUNKS):
        wq, vn, o = [], [], []
        for h in heads:
            gl = DN_HEADS + h
            qd = q[c, h] * eG[c][:, gl:gl + 1]
            wq.append(_dot(jnp.concatenate([sol[c, h][:, hd:], qd], axis=0).astype(BF16), S[h].astype(BF16)))
        for h in heads:
            vn.append((sol[c, h][:, :hd] - wq[h][:C]).astype(BF16))
        for h in heads:
            o.append(wq[h][C:] + _dot(attn[c, h], vn[h]))
        for h in heads:
            gl = DN_HEADS + h
            kd = k[c, h] * eGd[c][:, gl:gl + 1]
            S[h] = S[h] * eGl[c][:, gl:gl + 1] + _dot(kd.T.astype(BF16), vn[h])
        outs = []
        for h in heads:
            on = o[h] * lax.rsqrt(jnp.mean(o[h] * o[h], -1, keepdims=True) + NORM_EPS) * ng_ref[...]
            outs.append(on * _silu(z_ref[c * C:(c + 1) * C, h * hd:(h + 1) * hd]))
        o_ref[c * C:(c + 1) * C, :] = jnp.concatenate(outs, axis=1).astype(BF16)
    for h in heads:
        st_ref[h] = S[h]


def _deltanet(qkv, z, ba, conv_w, alog_l, dtb_l, norm_g, B, S):
    rows = DN_STEP_CHUNKS * CHUNK
    N = S // rows
    row = lambda w: pl.BlockSpec((rows, w), lambda b, n: (b * N + n, 0))
    const = lambda s: pl.BlockSpec(s, lambda b, n: (0, 0))
    return pl.pallas_call(
        _deltanet_kernel,
        out_shape=jax.ShapeDtypeStruct((B * S, DN_WIDTH), BF16),
        grid=(B, N),
        in_specs=[row(3 * DN_WIDTH), row(DN_WIDTH), row(LANES), const((CONV_WIDTH, 3 * DN_WIDTH)),
                  const((1, LANES)), const((1, LANES)), const((1, DN_HEAD_DIM))],
        out_specs=row(DN_WIDTH),
        scratch_shapes=[pltpu.VMEM((rows + SUBLANES, 3 * DN_WIDTH), F32),
                        pltpu.VMEM((DN_HEADS, DN_HEAD_DIM, DN_HEAD_DIM), F32)],
        compiler_params=_params(("parallel", "arbitrary")),
        name="gated_deltanet",
    )(qkv, z, ba, conv_w, alog_l, dtb_l, norm_g)


SB_TQ = 512
SB_TK = 256


def _stick_kernel(q_ref, k_ref, v_ref, o_ref):
    i = pl.program_id(2)
    tq, tk = SB_TQ, SB_TK
    ratio = tq // tk
    lane = lax.broadcasted_iota(I32, (tq, LANES), 1)
    r_io = lax.broadcasted_iota(I32, (tq, tk), 0)
    c_io = lax.broadcasted_iota(I32, (tq, tk), 1)
    u_r = lax.broadcasted_iota(I32, (2 * tk, tk), 0)
    u_c = lax.broadcasted_iota(I32, (2 * tk, tk), 1)
    later = jnp.where((u_r % tk) > u_c, 1.0, 0.0).astype(BF16)
    q = q_ref[...] * (SB_HEAD_DIM ** -0.5)
    zero = jnp.zeros_like(q)
    qs = [jnp.where((lane // SB_HEAD_DIM) == hh, q, zero) for hh in range(2)]

    def block(j, carry, masked):
        off = pl.multiple_of(j * tk, tk)
        kb = k_ref[pl.ds(off, tk), :]
        vb = v_ref[pl.ds(off, tk), :]
        if masked:
            causal = (c_io + j * tk) < (r_io + i * tq)
        zs = [_dot_nt(qs[hh], kb) for hh in range(2)]
        log_beta, log_rest, after = [], [], []
        for hh in range(2):
            lb = jnp.minimum(zs[hh], 0.0) - jnp.log(1.0 + jnp.exp(-jnp.abs(zs[hh])))
            lr = lb - zs[hh]
            if masked:
                lr = jnp.where(causal, lr, 0.0)
            log_beta.append(lb)
            log_rest.append(lr)
        for hh in range(2):
            hi, lo = _split_bf16(log_rest[hh])
            after.append(_dot(jnp.concatenate([hi, lo], axis=1), later))
        new = []
        for hh in range(2):
            rest, acc = carry[2 * hh], carry[2 * hh + 1]
            a = jnp.exp(log_beta[hh] + after[hh] + rest)
            if masked:
                a = jnp.where(causal, a, 0.0)
            acc = acc + _dot(a.astype(BF16), vb)
            rest = rest + (after[hh][:, 0:1] + log_rest[hh][:, 0:1])
            new += [rest, acc]
        return tuple(new)

    carry = (jnp.zeros((tq, 1), F32), jnp.zeros((tq, LANES), F32)) * 2
    for d in range(ratio - 1, -1, -1):
        carry = block(i * ratio + d, carry, True)
    n_full = i * ratio
    carry = lax.fori_loop(0, n_full, lambda t, c: block(n_full - 1 - t, c, False), carry)
    o_ref[...] = jnp.where(lane < SB_HEAD_DIM, carry[1], carry[3]).astype(BF16)


def _stick_breaking(sq, sk, sv, B, S):
    nq = S // SB_TQ
    pairs = SB_WIDTH // LANES
    q_spec = pl.BlockSpec((SB_TQ, LANES), lambda b, p, i: (b * nq + i, p))
    kv_spec = pl.BlockSpec((S, LANES), lambda b, p, i: (b, p))
    return pl.pallas_call(
        _stick_kernel,
        out_shape=jax.ShapeDtypeStruct((B * S, SB_WIDTH), BF16),
        grid=(B, pairs, nq),
        in_specs=[q_spec, kv_spec, kv_spec],
        out_specs=q_spec,
        compiler_params=_params(("parallel", "parallel", "arbitrary")),
        name="stick_breaking_attention",
    )(sq, sk, sv)


def _layer_norm(v, g, b):
    mu = jnp.mean(v, -1, keepdims=True)
    c = v - mu
    var = jnp.mean(c * c, -1, keepdims=True)
    return c * lax.rsqrt(var + LN_EPS) * g + b


def _silu(v):
    return v * jax.nn.sigmoid(v)


def _merge_kernel(oa_ref, ob_ref, ga_ref, gb_ref, x_ref, wpa_ref, wpb_ref, wout_ref, wr_ref, wsu_ref, wsd_ref,
                  g1_ref, b1_ref, h_ref, hs_ref, sc_ref):
    ya = _dot(oa_ref[...], wpa_ref[...])
    yb = _dot(ob_ref[...], wpb_ref[...])
    merged = jax.nn.sigmoid(ga_ref[...]) * ya + jax.nn.sigmoid(gb_ref[...]) * yb
    pre = DEEPNORM_ALPHA * x_ref[...] + _dot(merged.astype(BF16), wout_ref[...])
    h = _layer_norm(pre, g1_ref[...], b1_ref[...])
    _store_row_tiles(h_ref, h)
    hb = h.astype(BF16)
    sc_ref[...] = jax.nn.sigmoid(_dot(hb, wr_ref[...]))
    up = _dot(hb, wsu_ref[...])
    act = _silu(up[:, :SHARED_FF]) * up[:, SHARED_FF:]
    hs_ref[...] = DEEPNORM_ALPHA * h + _dot(act.astype(BF16), wsd_ref[...])


def _merge(oa, ob, ga, gb, xf, wpa, wpb, wout, wr, wsu, wsd, g1, b1, tm=256):
    T = xf.shape[0]
    row = lambda w: pl.BlockSpec((tm, w), lambda i: (i, 0))
    return pl.pallas_call(
        _merge_kernel,
        out_shape=[jax.ShapeDtypeStruct((T * ROW_TILE, LANES), U32), jax.ShapeDtypeStruct((T, D_MODEL), F32),
                   jax.ShapeDtypeStruct((T, N_EXPERTS), F32)],
        grid=(T // tm,),
        in_specs=[row(DN_WIDTH), row(SB_WIDTH), row(D_MODEL), row(D_MODEL), row(D_MODEL),
                  _const_spec(wpa.shape), _const_spec(wpb.shape), _const_spec(wout.shape), _const_spec(wr.shape),
                  _const_spec(wsu.shape), _const_spec(wsd.shape), _const_spec(g1.shape), _const_spec(b1.shape)],
        out_specs=[pl.BlockSpec((tm * ROW_TILE, LANES), lambda i: (i, 0)), row(D_MODEL), row(N_EXPERTS)],
        compiler_params=_params(("parallel",)),
        name="merge_norm_router_shared",
    )(oa, ob, ga, gb, xf, wpa, wpb, wout, wr, wsu, wsd, g1, b1)


def _route_kernel(sc_ref, bias_ref, idx_ref, gate_ref, rank_ref, cnt_ref, run_ref):
    i = pl.program_id(0)
    tm = sc_ref.shape[0]

    @pl.when(i == 0)
    def _():
        run_ref[...] = jnp.zeros_like(run_ref)

    scores = sc_ref[...]
    sel = scores + bias_ref[...]
    lane_i = lax.broadcasted_iota(I32, (tm, N_EXPERTS), 1)
    grp = lane_i // GROUP_SIZE
    lane = lane_i.astype(F32)
    neg = F32(-jnp.inf)

    def first_argmax(v):
        m = jnp.max(v, -1, keepdims=True)
        ix = jnp.min(jnp.where(v == m, lane, float(N_EXPERTS)), -1, keepdims=True)
        return m, ix

    gscore = []
    for g in range(N_GROUPS):
        gv = jnp.where(grp == g, sel, neg)
        m1, i1 = first_argmax(gv)
        m2 = jnp.max(jnp.where(lane == i1, neg, gv), -1, keepdims=True)
        gscore.append(m1 + m2)
    emask = jnp.zeros((tm, N_EXPERTS), jnp.bool_)
    for g in range(N_GROUPS):
        beaten = jnp.zeros((tm, 1), I32)
        for o in range(N_GROUPS):
            if o == g:
                continue
            wins = (gscore[o] > gscore[g]) | ((gscore[o] == gscore[g]) & (o < g))
            beaten = beaten + wins.astype(I32)
        emask = emask | ((grp == g) & (beaten < TOP_GROUPS))
    masked = jnp.where(emask, sel, neg)

    picked = jnp.zeros((tm, N_EXPERTS), jnp.bool_)
    idx_cols, score_cols, hots = [], [], []
    for _ in range(TOP_K):
        _, ix = first_argmax(masked)
        hot = lane == ix
        idx_cols.append(ix)
        score_cols.append(jnp.sum(jnp.where(hot, scores, 0.0), -1, keepdims=True))
        hots.append(hot)
        masked = jnp.where(hot, neg, masked)
        picked = picked | hot
    chosen = jnp.where(picked, 1.0, 0.0).astype(BF16)
    r_io = lax.broadcasted_iota(I32, (tm, tm), 0)
    c_io = lax.broadcasted_iota(I32, (tm, tm), 1)
    strict_lower = jnp.where(c_io < r_io, 1.0, 0.0).astype(BF16)
    before = _dot(strict_lower, chosen) + run_ref[...]
    run_ref[...] = run_ref[...] + jnp.sum(chosen.astype(F32), 0, keepdims=True)
    cnt_ref[...] = run_ref[...]

    total = score_cols[0]
    for s in score_cols[1:]:
        total = total + s
    lane_o = lax.broadcasted_iota(I32, (tm, LANES), 1)
    idx_o = jnp.zeros((tm, LANES), F32)
    gate_o = jnp.zeros((tm, LANES), F32)
    rank_o = jnp.zeros((tm, LANES), F32)
    for k in range(TOP_K):
        rk = jnp.sum(jnp.where(hots[k], before, 0.0), -1, keepdims=True)
        idx_o = jnp.where(lane_o == k, idx_cols[k], idx_o)
        gate_o = jnp.where(lane_o == k, score_cols[k] / total * ROUTED_SCALE, gate_o)
        rank_o = jnp.where(lane_o == k, rk, rank_o)
    idx_ref[...] = idx_o.astype(I32)
    gate_ref[...] = gate_o
    rank_ref[...] = rank_o.astype(I32)


def _route(scores, bias, tm=256):
    T = scores.shape[0]
    row = lambda w: pl.BlockSpec((tm, w), lambda i: (i, 0))
    return pl.pallas_call(
        _route_kernel,
        out_shape=[jax.ShapeDtypeStruct((T, LANES), I32), jax.ShapeDtypeStruct((T, LANES), F32),
                   jax.ShapeDtypeStruct((T, LANES), I32), jax.ShapeDtypeStruct((1, N_EXPERTS), F32)],
        grid=(T // tm,),
        in_specs=[row(N_EXPERTS), pl.BlockSpec((1, N_EXPERTS), lambda i: (0, 0))],
        out_specs=[row(LANES), row(LANES), row(LANES), pl.BlockSpec((1, N_EXPERTS), lambda i: (0, 0))],
        scratch_shapes=[pltpu.VMEM((1, N_EXPERTS), F32)],
        compiler_params=_params(("arbitrary",)),
        name="route_topk",
    )(scores, bias)


def _dest_kernel(idx_ref, rank_ref, pstart_ref, dest_ref):
    tm = idx_ref.shape[0]
    idx = idx_ref[...]
    lane = lax.broadcasted_iota(I32, (tm, N_EXPERTS), 1)
    lane_o = lax.broadcasted_iota(I32, (tm, LANES), 1)
    start = jnp.zeros((tm, LANES), F32)
    for k in range(TOP_K):
        sk = jnp.sum(jnp.where(lane == idx[:, k:k + 1], pstart_ref[...], 0.0), -1, keepdims=True)
        start = jnp.where(lane_o == k, sk, start)
    dest_ref[...] = start.astype(I32) + rank_ref[...]


def _dest_rows(idx, rank, pstart_f, tm=512):
    T = idx.shape[0]
    row = pl.BlockSpec((tm, LANES), lambda i: (i, 0))
    return pl.pallas_call(
        _dest_kernel,
        out_shape=jax.ShapeDtypeStruct((T, LANES), I32),
        grid=(T // tm,),
        in_specs=[row, row, pl.BlockSpec((1, N_EXPERTS), lambda i: (0, 0))],
        out_specs=row,
        compiler_params=_params(("arbitrary",)),
        name="dest_rows",
    )(idx, rank, pstart_f)


DISPATCH_TOKENS = 256
COMBINE_TOKENS = 128
EXPERT_ROWS = 256


def _dispatch_kernel(dest_ref, h_ref, xs_hbm, sem):
    def issue(t, carry):
        src = pl.multiple_of(t * ROW_TILE, ROW_TILE)
        for k in range(TOP_K):
            dst = pl.multiple_of(dest_ref[t * TOP_K + k] * ROW_TILE, ROW_TILE)
            pltpu.make_async_copy(h_ref.at[pl.ds(src, ROW_TILE)], xs_hbm.at[pl.ds(dst, ROW_TILE)],
                                  sem).start(priority=k % 2)
        return carry

    lax.fori_loop(0, DISPATCH_TOKENS, issue, 0)
    for _ in range(TOP_K):
        pltpu.make_async_copy(h_ref, xs_hbm.at[pl.ds(0, DISPATCH_TOKENS * ROW_TILE)], sem).wait()


def _dispatch(dest_flat, h_rt, n_rows):
    T = h_rt.shape[0] // ROW_TILE
    tm = DISPATCH_TOKENS
    return pl.pallas_call(
        _dispatch_kernel,
        out_shape=jax.ShapeDtypeStruct((n_rows * ROW_TILE, LANES), U32),
        grid=(T // tm,),
        in_specs=[pl.BlockSpec((tm * TOP_K,), lambda i: (i,), memory_space=pltpu.SMEM),
                  pl.BlockSpec((tm * ROW_TILE, LANES), lambda i: (i, 0))],
        out_specs=pl.BlockSpec(memory_space=pl.ANY),
        scratch_shapes=[pltpu.SemaphoreType.DMA(())],
        compiler_params=pltpu.CompilerParams(dimension_semantics=("arbitrary",), has_side_effects=True,
                                             vmem_limit_bytes=VMEM_LIMIT),
        name="dispatch_rows",
    )(dest_flat, h_rt)


def _expert_kernel(blk_ref, be_ref, first_ref, slot_ref, nexte_ref, nu_ref, xs_ref, wu_hbm, wd_hbm, ys_ref,
                   wu32_ref, wd32_ref, wu16_ref, wd16_ref, sems):
    b = pl.program_id(0)
    used = b < nu_ref[0]

    def weight_copies(e, s):
        return (pltpu.make_async_copy(wu_hbm.at[e], wu32_ref.at[s], sems.at[0, s]),
                pltpu.make_async_copy(wd_hbm.at[e], wd32_ref.at[s], sems.at[1, s]))

    @pl.when(b == 0)
    def _():
        for c in weight_copies(be_ref[0], 0):
            c.start()

    @pl.when(used & (first_ref[b] == 1))
    def _():
        s = slot_ref[b]
        for c in weight_copies(be_ref[b], s):
            c.wait()
        wu16_ref[...] = wu32_ref[s].astype(BF16)
        wd16_ref[...] = wd32_ref[s].astype(BF16)

        @pl.when(nexte_ref[b] >= 0)
        def _():
            for c in weight_copies(nexte_ref[b], 1 - s):
                c.start()

    @pl.when(used)
    def _():
        xb = _load_row_tiles(xs_ref, EXPERT_ROWS).astype(BF16)
        up = _dot(xb, wu16_ref[...])
        act = _silu(up[:, :EXPERT_FF]) * up[:, EXPERT_FF:]
        _store_row_tiles(ys_ref, _dot(act.astype(BF16), wd16_ref[...]))


def _experts(block_idx, block_e, first, slot, next_e, n_used, xs, w_up, w_down):
    n_blocks = xs.shape[0] // (EXPERT_ROWS * ROW_TILE)
    rows = pl.BlockSpec((EXPERT_ROWS * ROW_TILE, LANES), lambda b, bi, *_: (bi[b], 0))
    grid_spec = pltpu.PrefetchScalarGridSpec(
        num_scalar_prefetch=6,
        grid=(n_blocks,),
        in_specs=[rows, pl.BlockSpec(memory_space=pl.ANY), pl.BlockSpec(memory_space=pl.ANY)],
        out_specs=rows,
        scratch_shapes=[pltpu.VMEM((2, D_MODEL, 2 * EXPERT_FF), F32), pltpu.VMEM((2, EXPERT_FF, D_MODEL), F32),
                        pltpu.VMEM((D_MODEL, 2 * EXPERT_FF), BF16), pltpu.VMEM((EXPERT_FF, D_MODEL), BF16),
                        pltpu.SemaphoreType.DMA((2, 2))],
    )
    return pl.pallas_call(
        _expert_kernel,
        out_shape=jax.ShapeDtypeStruct(xs.shape, U32),
        grid_spec=grid_spec,
        compiler_params=_params(("arbitrary",)),
        name="routed_experts",
    )(block_idx, block_e, first, slot, next_e, n_used, xs, w_up, w_down)


def _combine_kernel(dest_ref, destn_ref, gate_ref, hs_ref, g2_ref, b2_ref, ys_hbm, out_ref, ybuf, sems):
    i = pl.program_id(0)
    n = pl.num_programs(0)
    slot = i % 2

    def gather(dr, s):
        def issue(t, carry):
            dst = pl.multiple_of(t * ROW_TILE, ROW_TILE)
            for k in range(TOP_K):
                src = pl.multiple_of(dr[t * TOP_K + k] * ROW_TILE, ROW_TILE)
                pltpu.make_async_copy(ys_hbm.at[pl.ds(src, ROW_TILE)], ybuf.at[s, k, pl.ds(dst, ROW_TILE)],
                                      sems.at[s]).start(priority=k % 2)
            return carry

        lax.fori_loop(0, COMBINE_TOKENS, issue, 0)

    @pl.when(i == 0)
    def _():
        gather(dest_ref, 0)

    @pl.when(i + 1 < n)
    def _():
        gather(destn_ref, 1 - slot)

    for k in range(TOP_K):
        pltpu.make_async_copy(ys_hbm.at[pl.ds(0, COMBINE_TOKENS * ROW_TILE)], ybuf.at[slot, k],
                              sems.at[slot]).wait()

    gates = gate_ref[...]
    acc = hs_ref[...]
    for k in range(TOP_K):
        acc = acc + gates[:, k:k + 1] * _load_row_tiles(ybuf, COMBINE_TOKENS, (slot, k))
    out_ref[...] = _layer_norm(acc, g2_ref[...], b2_ref[...])


def _combine(dest_flat, gates, hs, g2, b2, ys):
    T = hs.shape[0]
    tm = COMBINE_TOKENS
    n = T // tm
    cur = pl.BlockSpec((tm * TOP_K,), lambda i: (i,), memory_space=pltpu.SMEM)
    nxt = pl.BlockSpec((tm * TOP_K,), lambda i: (jnp.minimum(i + 1, n - 1),), memory_space=pltpu.SMEM)
    return pl.pallas_call(
        _combine_kernel,
        out_shape=jax.ShapeDtypeStruct((T, D_MODEL), F32),
        grid=(n,),
        in_specs=[cur, nxt, pl.BlockSpec((tm, LANES), lambda i: (i, 0)), pl.BlockSpec((tm, D_MODEL), lambda i: (i, 0)),
                  pl.BlockSpec((1, D_MODEL), lambda i: (0, 0)), pl.BlockSpec((1, D_MODEL), lambda i: (0, 0)),
                  pl.BlockSpec(memory_space=pl.ANY)],
        out_specs=pl.BlockSpec((tm, D_MODEL), lambda i: (i, 0)),
        scratch_shapes=[pltpu.VMEM((2, TOP_K, tm * ROW_TILE, LANES), U32), pltpu.SemaphoreType.DMA((2,))],
        compiler_params=_params(("arbitrary",)),
        name="combine_norm",
    )(dest_flat, dest_flat, gates, hs, g2, b2, ys)


def _pack_w_in(w_in):
    c1 = 4 * DN_WIDTH
    c2 = c1 + 2 * DN_HEADS
    ba = jnp.pad(w_in[:, c1:c2], ((0, 0), (0, LANES - 2 * DN_HEADS)))
    return jnp.concatenate([w_in[:, :c1], ba, w_in[:, c2:]], axis=1).astype(BF16)


def _head_lanes(v):
    return jnp.pad(v.astype(F32), (DN_HEADS, LANES - 2 * DN_HEADS)).reshape(1, LANES)


def _layer(x, w_in, conv_w, dn_a_log, dn_dt_bias, dn_norm_g, w_proj_a, w_proj_b, w_out, ln1_g, ln1_b, w_router,
           router_bias, w_shared_up, w_shared_down, w_expert_up, w_expert_down, ln2_g, ln2_b):
    B, S, D = x.shape
    T = B * S
    xf = x.reshape(T, D)
    qkv, z, ba, sq, sk, sv, ga, gb = _input_projection(xf, _pack_w_in(w_in))

    oa = _deltanet(qkv, z, ba, conv_w, _head_lanes(dn_a_log), _head_lanes(dn_dt_bias),
                   dn_norm_g.reshape(1, DN_HEAD_DIM), B, S)
    ob = _stick_breaking(sq, sk, sv, B, S)

    h, hs, scores = _merge(oa, ob, ga, gb, xf, w_proj_a.astype(BF16), w_proj_b.astype(BF16), w_out.astype(BF16),
                           w_router.astype(BF16), w_shared_up.astype(BF16), w_shared_down.astype(BF16),
                           ln1_g.reshape(1, D), ln1_b.reshape(1, D))

    idx, gates, rank, counts = _route(scores, router_bias.reshape(1, N_EXPERTS))

    out = _routed_path(h, hs, idx, gates, rank, counts, w_expert_up, w_expert_down, ln2_g.reshape(1, D),
                       ln2_b.reshape(1, D))
    return out.reshape(B, S, D)


def _routed_path(h, hs, idx, gates, rank, counts, w_expert_up, w_expert_down, g2, b2):
    T = hs.shape[0]
    counts = counts.reshape(N_EXPERTS).astype(I32)
    padded = (counts + EXPERT_ROWS - 1) // EXPERT_ROWS * EXPERT_ROWS
    pend = jnp.cumsum(padded)
    pstart = (pend - padded).astype(I32)
    n_blocks = -(-(T * TOP_K + N_EXPERTS * (EXPERT_ROWS - 1)) // EXPERT_ROWS)
    n_used = (pend[-1] // EXPERT_ROWS).astype(I32)
    block_idx = jnp.minimum(jnp.arange(n_blocks, dtype=I32), n_used - 1)
    block_e = jnp.minimum(jnp.sum(pend[None, :] <= (block_idx * EXPERT_ROWS)[:, None], 1), N_EXPERTS - 1).astype(I32)

    prev_e = jnp.concatenate([jnp.full((1,), -1, I32), block_e[:-1]])
    first = (block_e != prev_e).astype(I32)
    slot = ((jnp.cumsum(first) - 1) % 2).astype(I32)
    expert_ids = jnp.arange(N_EXPERTS, dtype=I32)
    later_used = jnp.where(counts > 0, expert_ids, N_EXPERTS)
    next_used = lax.cummin(jnp.concatenate([later_used[1:], jnp.full((1,), N_EXPERTS, I32)]), reverse=True)
    next_used = jnp.where(next_used < N_EXPERTS, next_used, -1)
    next_e = next_used[block_e].astype(I32)

    dest = _dest_rows(idx, rank, pstart.astype(F32).reshape(1, N_EXPERTS))
    dest_flat = dest[:, :TOP_K].reshape(T * TOP_K)
    xs = _dispatch(dest_flat, h, n_blocks * EXPERT_ROWS)
    ys = _experts(block_idx, block_e, first, slot, next_e, n_used.reshape(1), xs, w_expert_up, w_expert_down)
    return _combine(dest_flat, gates, hs, g2, b2, ys)


def kernel(x, w_in, conv_w, dn_a_log, dn_dt_bias, dn_norm_g, w_proj_a, w_proj_b, w_out, ln1_g, ln1_b, w_router,
           router_bias, w_shared_up, w_shared_down, w_expert_up, w_expert_down, ln2_g, ln2_b):
    for layer in range(DEPTH):
        x = _layer(x, w_in[layer], conv_w[layer], dn_a_log[layer], dn_dt_bias[layer], dn_norm_g[layer],
                   w_proj_a[layer], w_proj_b[layer], w_out[layer], ln1_g[layer], ln1_b[layer], w_router[layer],
                   router_bias[layer], w_shared_up[layer], w_shared_down[layer], w_expert_up[layer],
                   w_expert_down[layer], ln2_g[layer], ln2_b[layer])
    return x
```

```python
import functools
import math

import jax
import jax.numpy as jnp
import numpy as np
from jax import lax
from jax.experimental import pallas as pl
from jax.experimental.pallas import tpu as pltpu

F32 = jnp.float32
BF16 = jnp.bfloat16
I32 = jnp.int32

D_MODEL = 1024
CHUNK = 64
DN_HEADS = 4
DN_HEAD_DIM = 128
DN_WIDTH = DN_HEADS * DN_HEAD_DIM
CONV_WIDTH = 4
SB_HEADS = 8
SB_HEAD_DIM = 64
SB_WIDTH = SB_HEADS * SB_HEAD_DIM
N_EXPERTS = 256
TOP_K = 8
N_GROUPS = 8
GROUP_SIZE = N_EXPERTS // N_GROUPS
TOP_GROUPS = 4
EXPERT_FF = 256
SHARED_FF = 256
ROUTED_SCALE = 2.5
EXPERT_BLOCK = 128
DEPTH = 1
DEEPNORM_ALPHA = (2 * DEPTH) ** 0.25
LN_EPS = 1e-5
NORM_EPS = 1e-6

LANES = 128
SUBLANES = 8
VMEM_LIMIT = 56 * 1024 * 1024

_C_QKV = (0, 3 * DN_WIDTH)
_C_Z = (_C_QKV[1], _C_QKV[1] + DN_WIDTH)
_C_BA = (_C_Z[1], _C_Z[1] + LANES)
_C_SQ = (_C_BA[1], _C_BA[1] + SB_WIDTH)
_C_SK = (_C_SQ[1], _C_SQ[1] + SB_WIDTH)
_C_SV = (_C_SK[1], _C_SK[1] + SB_WIDTH)
_C_GA = (_C_SV[1], _C_SV[1] + D_MODEL)
_C_GB = (_C_GA[1], _C_GA[1] + D_MODEL)
PROJ_PACKED = _C_GB[1]


def _params(sem, vmem=VMEM_LIMIT):
    return pltpu.CompilerParams(dimension_semantics=sem, vmem_limit_bytes=vmem)


def _const_spec(shape):
    return pl.BlockSpec(shape, lambda *_: (0,) * len(shape), pipeline_mode=pl.Buffered(1))


def _dot(a, b):
    return jnp.dot(a, b, preferred_element_type=F32)


def _dot_nt(a, b):
    return lax.dot_general(a, b, (((1,), (1,)), ((), ())), preferred_element_type=F32)


def _split_bf16(a):
    hi = a.astype(BF16)
    return hi, (a - hi.astype(F32)).astype(BF16)


U32 = jnp.uint32
HALF_MODEL = D_MODEL // 2
ROW_TILE = HALF_MODEL // LANES
HIGH_HALF_MASK = 0xFFFF0000


def _store_row_tiles(ref, mat):
    n = mat.shape[0]
    lo = lax.bitcast_convert_type(mat[:, :HALF_MODEL].astype(BF16).astype(F32), U32)
    hi = lax.bitcast_convert_type(mat[:, HALF_MODEL:].astype(BF16).astype(F32), U32)
    words = (lo >> 16) | hi
    for s in range(ROW_TILE):
        ref[pl.ds(s, n, stride=ROW_TILE), :] = words[:, s * LANES:(s + 1) * LANES]


def _load_row_tiles(ref, n, lead=()):
    words = [ref[lead + (pl.ds(s, n, stride=ROW_TILE), slice(None))] for s in range(ROW_TILE)]
    lo = [lax.bitcast_convert_type(w << 16, F32) for w in words]
    hi = [lax.bitcast_convert_type(w & U32(HIGH_HALF_MASK), F32) for w in words]
    return jnp.concatenate(lo + hi, axis=1)


def _proj_kernel(x_ref, w_ref, qkv_ref, z_ref, ba_ref, sq_ref, sk_ref, sv_ref, ga_ref, gb_ref):
    xb = x_ref[...].astype(BF16)

    def mm(c):
        return _dot(xb, w_ref[:, c[0]:c[1]])

    qkv_ref[...] = mm(_C_QKV)
    z_ref[...] = mm(_C_Z).astype(BF16)
    ba_ref[...] = mm(_C_BA)
    sq_ref[...] = mm(_C_SQ).astype(BF16)
    sk_ref[...] = mm(_C_SK).astype(BF16)
    sv_ref[...] = mm(_C_SV).astype(BF16)
    ga_ref[...] = mm(_C_GA).astype(BF16)
    gb_ref[...] = mm(_C_GB).astype(BF16)


def _input_projection(xf, w_packed, tm=256):
    T = xf.shape[0]
    widths = [(3 * DN_WIDTH, F32), (DN_WIDTH, BF16), (LANES, F32), (SB_WIDTH, BF16), (SB_WIDTH, BF16),
              (SB_WIDTH, BF16), (D_MODEL, BF16), (D_MODEL, BF16)]
    return pl.pallas_call(
        _proj_kernel,
        out_shape=[jax.ShapeDtypeStruct((T, w), dt) for w, dt in widths],
        grid=(T // tm,),
        in_specs=[pl.BlockSpec((tm, D_MODEL), lambda i: (i, 0)), _const_spec((D_MODEL, PROJ_PACKED))],
        out_specs=[pl.BlockSpec((tm, w), lambda i: (i, 0)) for w, _ in widths],
        compiler_params=_params(("parallel",)),
        name="input_projection",
    )(xf, w_packed)


def _mm3(a, b):
    ah, al = _split_bf16(a)
    bh, bl = _split_bf16(b)
    return _dot(ah, bh) + _dot(ah, bl) + _dot(al, bh)


DN_STEP_CHUNKS = 4


def _deltanet_kernel(qkv_ref, z_ref, ba_ref, cw_ref, alog_ref, dtb_ref, ng_ref, o_ref, win_ref, st_ref):
    n = pl.program_id(1)
    C = CHUNK
    R = DN_STEP_CHUNKS * C
    hd = DN_HEAD_DIM

    @pl.when(n == 0)
    def _():
        win_ref[0:SUBLANES, :] = jnp.zeros((SUBLANES, 3 * DN_WIDTH), F32)
        st_ref[...] = jnp.zeros_like(st_ref)

    win_ref[SUBLANES:SUBLANES + R, :] = qkv_ref[...]
    conv = win_ref[pl.ds(SUBLANES - (CONV_WIDTH - 1), R), :] * cw_ref[0:1, :]
    for tap in range(1, CONV_WIDTH):
        conv = conv + win_ref[pl.ds(SUBLANES - (CONV_WIDTH - 1) + tap, R), :] * cw_ref[tap:tap + 1, :]
    win_ref[0:SUBLANES, :] = win_ref[R:R + SUBLANES, :]
    act_all = _silu(conv)

    bt = ba_ref[...]
    beta_rows = jax.nn.sigmoid(bt)
    g_rows = -jnp.exp(alog_ref[...]) * jax.nn.softplus(bt + dtb_ref[...])
    r_io = lax.broadcasted_iota(I32, (C, C), 0)
    c_io = lax.broadcasted_iota(I32, (C, C), 1)
    incl = r_io >= c_io
    strict = r_io > c_io
    tri = jnp.where(incl, 1.0, 0.0).astype(BF16)
    eye = jnp.where(r_io == c_io, 1.0, 0.0).astype(F32)
    pairs = [(c, h) for c in range(DN_STEP_CHUNKS) for h in range(DN_HEADS)]
    G, eG, eGd, eGl, Gt = [], [], [], [], []
    for c in range(DN_STEP_CHUNKS):
        g_hi, g_lo = _split_bf16(g_rows[c * C:(c + 1) * C])
        G.append(_dot(tri, g_hi) + _dot(tri, g_lo))
    for c in range(DN_STEP_CHUNKS):
        G_last = G[c][C - 1:C, :]
        eG.append(jnp.exp(G[c]))
        eGd.append(jnp.exp(G_last - G[c]))
        eGl.append(jnp.exp(G_last))
        Gt.append(jnp.concatenate([G[c], jnp.zeros((LANES - C, LANES), F32)], axis=0).T)

    q, k, k16, kb, v, beta, decay = {}, {}, {}, {}, {}, {}, {}
    for c, h in pairs:
        act = act_all[c * C:(c + 1) * C]
        qq = act[:, h * hd:(h + 1) * hd]
        kk = act[:, DN_WIDTH + h * hd:DN_WIDTH + (h + 1) * hd]
        v[c, h] = act[:, 2 * DN_WIDTH + h * hd:2 * DN_WIDTH + (h + 1) * hd]
        q[c, h] = qq * lax.rsqrt(jnp.sum(qq * qq, -1, keepdims=True) + NORM_EPS) * (hd ** -0.5)
        k[c, h] = kk * lax.rsqrt(jnp.sum(kk * kk, -1, keepdims=True) + NORM_EPS)
        beta[c, h] = beta_rows[c * C:(c + 1) * C, h:h + 1]
        gl = DN_HEADS + h
        diff = G[c][:, gl:gl + 1] - Gt[c][gl:gl + 1, 0:C]
        decay[c, h] = jnp.where(incl, jnp.exp(jnp.where(incl, diff, 0.0)), 0.0)
        kb[c, h] = k[c, h] * beta[c, h]
        k16[c, h] = k[c, h].astype(BF16)
    M = {p: jnp.where(strict, _dot_nt(kb[p].astype(BF16), k16[p]) * decay[p], 0.0) for p in pairs}
    attn = {p: (_dot_nt(q[p].astype(BF16), k16[p]) * decay[p]).astype(BF16) for p in pairs}
    P = {p: eye - M[p] for p in pairs}
    Mp = M
    for _ in range(5):
        Mp = {p: _mm3(Mp[p], Mp[p]) for p in pairs}
        P = {p: P[p] + _mm3(P[p], Mp[p]) for p in pairs}
    sol = {}
    for c, h in pairs:
        gl = DN_HEADS + h
        rhs = jnp.concatenate([v[c, h] * beta[c, h], kb[c, h] * eG[c][:, gl:gl + 1]], axis=1)
        sol[c, h] = _mm3(P[c, h], rhs)

    heads = range(DN_HEADS)
    S = [st_ref[h] for h in heads]
    for c in range(DN_STEP_CHUNKS):
        wq, vn, o = [], [], []
        for h in heads:
            gl = DN_HEADS + h
            qd = q[c, h] * eG[c][:, gl:gl + 1]
            wq.append(_dot(jnp.concatenate([sol[c, h][:, hd:], qd], axis=0).astype(BF16), S[h].astype(BF16)))
        for h in heads:
            vn.append((sol[c, h][:, :hd] - wq[h][:C]).astype(BF16))
        for h in heads:
            o.append(wq[h][C:] + _dot(attn[c, h], vn[h]))
        for h in heads:
            gl = DN_HEADS + h
            kd = k[c, h] * eGd[c][:, gl:gl + 1]
            S[h] = S[h] * eGl[c][:, gl:gl + 1] + _dot(kd.T.astype(BF16), vn[h])
        outs = []
        for h in heads:
            on = o[h] * lax.rsqrt(jnp.mean(o[h] * o[h], -1, keepdims=True) + NORM_EPS) * ng_ref[...]
            outs.append(on * _silu(z_ref[c * C:(c + 1) * C, h * hd:(h + 1) * hd].astype(F32)))
        o_ref[c * C:(c + 1) * C, :] = jnp.concatenate(outs, axis=1).astype(BF16)
    for h in heads:
        st_ref[h] = S[h]


def _deltanet(qkv, z, ba, conv_w, alog_l, dtb_l, norm_g, B, S):
    rows = DN_STEP_CHUNKS * CHUNK
    N = S // rows
    row = lambda w: pl.BlockSpec((rows, w), lambda b, n: (b * N + n, 0))
    const = lambda s: pl.BlockSpec(s, lambda b, n: (0, 0))
    return pl.pallas_call(
        _deltanet_kernel,
        out_shape=jax.ShapeDtypeStruct((B * S, DN_WIDTH), BF16),
        grid=(B, N),
        in_specs=[row(3 * DN_WIDTH), row(DN_WIDTH), row(LANES), const((CONV_WIDTH, 3 * DN_WIDTH)),
                  const((1, LANES)), const((1, LANES)), const((1, DN_HEAD_DIM))],
        out_specs=row(DN_WIDTH),
        scratch_shapes=[pltpu.VMEM((rows + SUBLANES, 3 * DN_WIDTH), F32),
                        pltpu.VMEM((DN_HEADS, DN_HEAD_DIM, DN_HEAD_DIM), F32)],
        compiler_params=_params(("parallel", "arbitrary")),
        name="gated_deltanet",
    )(qkv, z, ba, conv_w, alog_l, dtb_l, norm_g)


SB_TQ = 512
SB_TK = 256


def _stick_kernel(q_ref, k_ref, v_ref, o_ref):
    i = pl.program_id(2)
    tq, tk = SB_TQ, SB_TK
    ratio = tq // tk
    lane = lax.broadcasted_iota(I32, (tq, LANES), 1)
    r_io = lax.broadcasted_iota(I32, (tq, tk), 0)
    c_io = lax.broadcasted_iota(I32, (tq, tk), 1)
    u_r = lax.broadcasted_iota(I32, (2 * tk, tk), 0)
    u_c = lax.broadcasted_iota(I32, (2 * tk, tk), 1)
    later = jnp.where((u_r % tk) > u_c, 1.0, 0.0).astype(BF16)
    q = q_ref[...] * (SB_HEAD_DIM ** -0.5)
    zero = jnp.zeros_like(q)
    qs = [jnp.where((lane // SB_HEAD_DIM) == hh, q, zero) for hh in range(2)]

    def block(j, carry, masked):
        off = pl.multiple_of(j * tk, tk)
        kb = k_ref[pl.ds(off, tk), :]
        vb = v_ref[pl.ds(off, tk), :]
        if masked:
            causal = (c_io + j * tk) < (r_io + i * tq)
        zs = [_dot_nt(qs[hh], kb) for hh in range(2)]
        log_beta, log_rest, after = [], [], []
        for hh in range(2):
            lb = jnp.minimum(zs[hh], 0.0) - jnp.log(1.0 + jnp.exp(-jnp.abs(zs[hh])))
            lr = lb - zs[hh]
            if masked:
                lr = jnp.where(causal, lr, 0.0)
            log_beta.append(lb)
            log_rest.append(lr)
        for hh in range(2):
            hi, lo = _split_bf16(log_rest[hh])
            after.append(_dot(jnp.concatenate([hi, lo], axis=1), later))
        new = []
        for hh in range(2):
            rest, acc = carry[2 * hh], carry[2 * hh + 1]
            a = jnp.exp(log_beta[hh] + after[hh] + rest)
            if masked:
                a = jnp.where(causal, a, 0.0)
            acc = acc + _dot(a.astype(BF16), vb)
            rest = rest + (after[hh][:, 0:1] + log_rest[hh][:, 0:1])
            new += [rest, acc]
        return tuple(new)

    carry = (jnp.zeros((tq, 1), F32), jnp.zeros((tq, LANES), F32)) * 2
    for d in range(ratio - 1, -1, -1):
        carry = block(i * ratio + d, carry, True)
    n_full = i * ratio
    carry = lax.fori_loop(0, n_full, lambda t, c: block(n_full - 1 - t, c, False), carry)
    o_ref[...] = jnp.where(lane < SB_HEAD_DIM, carry[1], carry[3]).astype(BF16)


def _stick_breaking(sq, sk, sv, B, S):
    nq = S // SB_TQ
    pairs = SB_WIDTH // LANES
    q_spec = pl.BlockSpec((SB_TQ, LANES), lambda b, p, i: (b * nq + i, p))
    kv_spec = pl.BlockSpec((S, LANES), lambda b, p, i: (b, p))
    return pl.pallas_call(
        _stick_kernel,
        out_shape=jax.ShapeDtypeStruct((B * S, SB_WIDTH), BF16),
        grid=(B, pairs, nq),
        in_specs=[q_spec, kv_spec, kv_spec],
        out_specs=q_spec,
        compiler_params=_params(("parallel", "parallel", "arbitrary")),
        name="stick_breaking_attention",
    )(sq, sk, sv)


def _layer_norm(v, g, b):
    mu = jnp.mean(v, -1, keepdims=True)
    c = v - mu
    var = jnp.mean(c * c, -1, keepdims=True)
    return c * lax.rsqrt(var + LN_EPS) * g + b


def _silu(v):
    return v * jax.nn.sigmoid(v)


def _merge_kernel(oa_ref, ob_ref, ga_ref, gb_ref, x_ref, wpa_ref, wpb_ref, wout_ref, wr_ref, wsu_ref, wsd_ref,
                  g1_ref, b1_ref, rbias_ref, h_ref, hs_ref, idx_ref, gate_ref, rank_ref, cnt_ref, run_ref):
    ya = _dot(oa_ref[...], wpa_ref[...])
    yb = _dot(ob_ref[...], wpb_ref[...])
    merged = jax.nn.sigmoid(ga_ref[...].astype(F32)) * ya + jax.nn.sigmoid(gb_ref[...].astype(F32)) * yb
    pre = DEEPNORM_ALPHA * x_ref[...] + _dot(merged.astype(BF16), wout_ref[...])
    h = _layer_norm(pre, g1_ref[...], b1_ref[...])
    _store_row_tiles(h_ref, h)
    hb = h.astype(BF16)
    scores = jax.nn.sigmoid(_dot(hb, wr_ref[...]))
    up = _dot(hb, wsu_ref[...])
    act = _silu(up[:, :SHARED_FF]) * up[:, SHARED_FF:]
    hs_ref[...] = DEEPNORM_ALPHA * h + _dot(act.astype(BF16), wsd_ref[...])
    _route_tile(scores, rbias_ref, idx_ref, gate_ref, rank_ref, cnt_ref, run_ref)


def _merge(oa, ob, ga, gb, xf, wpa, wpb, wout, wr, wsu, wsd, g1, b1, rbias, tm=256):
    T = xf.shape[0]
    row = lambda w: pl.BlockSpec((tm, w), lambda i: (i, 0))
    return pl.pallas_call(
        _merge_kernel,
        out_shape=[jax.ShapeDtypeStruct((T * ROW_TILE, LANES), U32), jax.ShapeDtypeStruct((T, D_MODEL), F32),
                   jax.ShapeDtypeStruct((T, LANES), I32), jax.ShapeDtypeStruct((T, LANES), F32),
                   jax.ShapeDtypeStruct((T, LANES), I32), jax.ShapeDtypeStruct((1, N_EXPERTS), F32)],
        grid=(T // tm,),
        in_specs=[row(DN_WIDTH), row(SB_WIDTH), row(D_MODEL), row(D_MODEL), row(D_MODEL),
                  _const_spec(wpa.shape), _const_spec(wpb.shape), _const_spec(wout.shape), _const_spec(wr.shape),
                  _const_spec(wsu.shape), _const_spec(wsd.shape), _const_spec(g1.shape), _const_spec(b1.shape),
                  _const_spec(rbias.shape)],
        out_specs=[pl.BlockSpec((tm * ROW_TILE, LANES), lambda i: (i, 0)), row(D_MODEL), row(LANES), row(LANES),
                   row(LANES), pl.BlockSpec((1, N_EXPERTS), lambda i: (0, 0))],
        scratch_shapes=[pltpu.VMEM((1, N_EXPERTS), F32)],
        compiler_params=_params(("arbitrary",)),
        name="merge_norm_shared_route",
    )(oa, ob, ga, gb, xf, wpa, wpb, wout, wr, wsu, wsd, g1, b1, rbias)


def _route_tile(scores, bias_ref, idx_ref, gate_ref, rank_ref, cnt_ref, run_ref):
    i = pl.program_id(0)
    tm = scores.shape[0]

    @pl.when(i == 0)
    def _():
        run_ref[...] = jnp.zeros_like(run_ref)

    sel = scores + bias_ref[...]
    lane_i = lax.broadcasted_iota(I32, (tm, N_EXPERTS), 1)
    grp = lane_i // GROUP_SIZE
    lane = lane_i.astype(F32)
    neg = F32(-jnp.inf)

    def first_argmax(v):
        m = jnp.max(v, -1, keepdims=True)
        ix = jnp.min(jnp.where(v == m, lane, float(N_EXPERTS)), -1, keepdims=True)
        return m, ix

    gscore = []
    for g in range(N_GROUPS):
        gv = jnp.where(grp == g, sel, neg)
        m1, i1 = first_argmax(gv)
        m2 = jnp.max(jnp.where(lane == i1, neg, gv), -1, keepdims=True)
        gscore.append(m1 + m2)
    emask = jnp.zeros((tm, N_EXPERTS), jnp.bool_)
    for g in range(N_GROUPS):
        beaten = jnp.zeros((tm, 1), I32)
        for o in range(N_GROUPS):
            if o == g:
                continue
            wins = (gscore[o] > gscore[g]) | ((gscore[o] == gscore[g]) & (o < g))
            beaten = beaten + wins.astype(I32)
        emask = emask | ((grp == g) & (beaten < TOP_GROUPS))
    masked = jnp.where(emask, sel, neg)

    picked = jnp.zeros((tm, N_EXPERTS), jnp.bool_)
    idx_cols, score_cols, hots = [], [], []
    for _ in range(TOP_K):
        _, ix = first_argmax(masked)
        hot = lane == ix
        idx_cols.append(ix)
        score_cols.append(jnp.sum(jnp.where(hot, scores, 0.0), -1, keepdims=True))
        hots.append(hot)
        masked = jnp.where(hot, neg, masked)
        picked = picked | hot
    chosen = jnp.where(picked, 1.0, 0.0).astype(BF16)
    r_io = lax.broadcasted_iota(I32, (tm, tm), 0)
    c_io = lax.broadcasted_iota(I32, (tm, tm), 1)
    strict_lower = jnp.where(c_io < r_io, 1.0, 0.0).astype(BF16)
    before = _dot(strict_lower, chosen) + run_ref[...]
    run_ref[...] = run_ref[...] + jnp.sum(chosen.astype(F32), 0, keepdims=True)
    cnt_ref[...] = run_ref[...]

    total = score_cols[0]
    for s in score_cols[1:]:
        total = total + s
    lane_o = lax.broadcasted_iota(I32, (tm, LANES), 1)
    idx_o = jnp.zeros((tm, LANES), F32)
    gate_o = jnp.zeros((tm, LANES), F32)
    rank_o = jnp.zeros((tm, LANES), F32)
    for k in range(TOP_K):
        rk = jnp.sum(jnp.where(hots[k], before, 0.0), -1, keepdims=True)
        idx_o = jnp.where(lane_o == k, idx_cols[k], idx_o)
        gate_o = jnp.where(lane_o == k, score_cols[k] / total * ROUTED_SCALE, gate_o)
        rank_o = jnp.where(lane_o == k, rk, rank_o)
    idx_ref[...] = idx_o.astype(I32)
    gate_ref[...] = gate_o
    rank_ref[...] = rank_o.astype(I32)


def _dest_kernel(idx_ref, rank_ref, pstart_ref, dest_ref):
    tm = idx_ref.shape[0]
    idx = idx_ref[...]
    lane = lax.broadcasted_iota(I32, (tm, N_EXPERTS), 1)
    lane_o = lax.broadcasted_iota(I32, (tm, LANES), 1)
    start = jnp.zeros((tm, LANES), F32)
    for k in range(TOP_K):
        sk = jnp.sum(jnp.where(lane == idx[:, k:k + 1], pstart_ref[...], 0.0), -1, keepdims=True)
        start = jnp.where(lane_o == k, sk, start)
    dest_ref[...] = start.astype(I32) + rank_ref[...]


def _dest_rows(idx, rank, pstart_f, tm=512):
    T = idx.shape[0]
    row = pl.BlockSpec((tm, LANES), lambda i: (i, 0))
    return pl.pallas_call(
        _dest_kernel,
        out_shape=jax.ShapeDtypeStruct((T, LANES), I32),
        grid=(T // tm,),
        in_specs=[row, row, pl.BlockSpec((1, N_EXPERTS), lambda i: (0, 0))],
        out_specs=row,
        compiler_params=_params(("arbitrary",)),
        name="dest_rows",
    )(idx, rank, pstart_f)


DISPATCH_TOKENS = 256
COMBINE_TOKENS = 128
EXPERT_ROWS = 256


def _dispatch_kernel(dest_ref, h_ref, xs_hbm, sem):
    def issue(t, carry):
        src = pl.multiple_of(t * ROW_TILE, ROW_TILE)
        for k in range(TOP_K):
            dst = pl.multiple_of(dest_ref[t * TOP_K + k] * ROW_TILE, ROW_TILE)
            pltpu.make_async_copy(h_ref.at[pl.ds(src, ROW_TILE)], xs_hbm.at[pl.ds(dst, ROW_TILE)],
                                  sem).start(priority=k % 2)
        return carry

    lax.fori_loop(0, DISPATCH_TOKENS, issue, 0)
    for _ in range(TOP_K):
        pltpu.make_async_copy(h_ref, xs_hbm.at[pl.ds(0, DISPATCH_TOKENS * ROW_TILE)], sem).wait()


def _dispatch(dest_flat, h_rt, n_rows):
    T = h_rt.shape[0] // ROW_TILE
    tm = DISPATCH_TOKENS
    return pl.pallas_call(
        _dispatch_kernel,
        out_shape=jax.ShapeDtypeStruct((n_rows * ROW_TILE, LANES), U32),
        grid=(T // tm,),
        in_specs=[pl.BlockSpec((tm * TOP_K,), lambda i: (i,), memory_space=pltpu.SMEM),
                  pl.BlockSpec((tm * ROW_TILE, LANES), lambda i: (i, 0))],
        out_specs=pl.BlockSpec(memory_space=pl.ANY),
        scratch_shapes=[pltpu.SemaphoreType.DMA(())],
        compiler_params=pltpu.CompilerParams(dimension_semantics=("arbitrary",), has_side_effects=True,
                                             vmem_limit_bytes=VMEM_LIMIT),
        name="dispatch_rows",
    )(dest_flat, h_rt)


def _expert_kernel(blk_ref, be_ref, first_ref, slot_ref, nexte_ref, nu_ref, xs_ref, wu_hbm, wd_hbm, ys_ref,
                   wu32_ref, wd32_ref, wu16_ref, wd16_ref, sems):
    b = pl.program_id(0)
    used = b < nu_ref[0]

    def weight_copies(e, s):
        return (pltpu.make_async_copy(wu_hbm.at[e], wu32_ref.at[s], sems.at[0, s]),
                pltpu.make_async_copy(wd_hbm.at[e], wd32_ref.at[s], sems.at[1, s]))

    @pl.when(b == 0)
    def _():
        for c in weight_copies(be_ref[0], 0):
            c.start()

    @pl.when(used & (first_ref[b] == 1))
    def _():
        s = slot_ref[b]
        for c in weight_copies(be_ref[b], s):
            c.wait()
        wu16_ref[...] = wu32_ref[s].astype(BF16)
        wd16_ref[...] = wd32_ref[s].astype(BF16)

        @pl.when(nexte_ref[b] >= 0)
        def _():
            for c in weight_copies(nexte_ref[b], 1 - s):
                c.start()

    @pl.when(used)
    def _():
        xb = _load_row_tiles(xs_ref, EXPERT_ROWS).astype(BF16)
        up = _dot(xb, wu16_ref[...])
        act = _silu(up[:, :EXPERT_FF]) * up[:, EXPERT_FF:]
        _store_row_tiles(ys_ref, _dot(act.astype(BF16), wd16_ref[...]))


def _experts(block_idx, block_e, first, slot, next_e, n_used, xs, w_up, w_down):
    n_blocks = xs.shape[0] // (EXPERT_ROWS * ROW_TILE)
    rows = pl.BlockSpec((EXPERT_ROWS * ROW_TILE, LANES), lambda b, bi, *_: (bi[b], 0))
    grid_spec = pltpu.PrefetchScalarGridSpec(
        num_scalar_prefetch=6,
        grid=(n_blocks,),
        in_specs=[rows, pl.BlockSpec(memory_space=pl.ANY), pl.BlockSpec(memory_space=pl.ANY)],
        out_specs=rows,
        scratch_shapes=[pltpu.VMEM((2, D_MODEL, 2 * EXPERT_FF), F32), pltpu.VMEM((2, EXPERT_FF, D_MODEL), F32),
                        pltpu.VMEM((D_MODEL, 2 * EXPERT_FF), BF16), pltpu.VMEM((EXPERT_FF, D_MODEL), BF16),
                        pltpu.SemaphoreType.DMA((2, 2))],
    )
    return pl.pallas_call(
        _expert_kernel,
        out_shape=jax.ShapeDtypeStruct(xs.shape, U32),
        grid_spec=grid_spec,
        compiler_params=_params(("arbitrary",)),
        name="routed_experts",
    )(block_idx, block_e, first, slot, next_e, n_used, xs, w_up, w_down)


def _combine_kernel(dest_ref, destn_ref, gate_ref, hs_ref, g2_ref, b2_ref, ys_hbm, out_ref, ybuf, sems):
    i = pl.program_id(0)
    n = pl.num_programs(0)
    slot = i % 2

    def gather(dr, s):
        def issue(t, carry):
            dst = pl.multiple_of(t * ROW_TILE, ROW_TILE)
            for k in range(TOP_K):
                src = pl.multiple_of(dr[t * TOP_K + k] * ROW_TILE, ROW_TILE)
                pltpu.make_async_copy(ys_hbm.at[pl.ds(src, ROW_TILE)], ybuf.at[s, k, pl.ds(dst, ROW_TILE)],
                                      sems.at[s]).start(priority=k % 2)
            return carry

        lax.fori_loop(0, COMBINE_TOKENS, issue, 0)

    @pl.when(i == 0)
    def _():
        gather(dest_ref, 0)

    @pl.when(i + 1 < n)
    def _():
        gather(destn_ref, 1 - slot)

    for k in range(TOP_K):
        pltpu.make_async_copy(ys_hbm.at[pl.ds(0, COMBINE_TOKENS * ROW_TILE)], ybuf.at[slot, k],
                              sems.at[slot]).wait()

    gates = gate_ref[...]
    acc = hs_ref[...]
    for k in range(TOP_K):
        acc = acc + gates[:, k:k + 1] * _load_row_tiles(ybuf, COMBINE_TOKENS, (slot, k))
    out_ref[...] = _layer_norm(acc, g2_ref[...], b2_ref[...])


def _combine(dest_flat, gates, hs, g2, b2, ys):
    T = hs.shape[0]
    tm = COMBINE_TOKENS
    n = T // tm
    cur = pl.BlockSpec((tm * TOP_K,), lambda i: (i,), memory_space=pltpu.SMEM)
    nxt = pl.BlockSpec((tm * TOP_K,), lambda i: (jnp.minimum(i + 1, n - 1),), memory_space=pltpu.SMEM)
    return pl.pallas_call(
        _combine_kernel,
        out_shape=jax.ShapeDtypeStruct((T, D_MODEL), F32),
        grid=(n,),
        in_specs=[cur, nxt, pl.BlockSpec((tm, LANES), lambda i: (i, 0)), pl.BlockSpec((tm, D_MODEL), lambda i: (i, 0)),
                  pl.BlockSpec((1, D_MODEL), lambda i: (0, 0)), pl.BlockSpec((1, D_MODEL), lambda i: (0, 0)),
                  pl.BlockSpec(memory_space=pl.ANY)],
        out_specs=pl.BlockSpec((tm, D_MODEL), lambda i: (i, 0)),
        scratch_shapes=[pltpu.VMEM((2, TOP_K, tm * ROW_TILE, LANES), U32), pltpu.SemaphoreType.DMA((2,))],
        compiler_params=_params(("arbitrary",)),
        name="combine_norm",
    )(dest_flat, dest_flat, gates, hs, g2, b2, ys)


def _pack_w_in(w_in):
    c1 = 4 * DN_WIDTH
    c2 = c1 + 2 * DN_HEADS
    ba = jnp.pad(w_in[:, c1:c2], ((0, 0), (0, LANES - 2 * DN_HEADS)))
    return jnp.concatenate([w_in[:, :c1], ba, w_in[:, c2:]], axis=1).astype(BF16)


def _head_lanes(v):
    return jnp.pad(v.astype(F32), (DN_HEADS, LANES - 2 * DN_HEADS)).reshape(1, LANES)


def _layer(x, w_in, conv_w, dn_a_log, dn_dt_bias, dn_norm_g, w_proj_a, w_proj_b, w_out, ln1_g, ln1_b, w_router,
           router_bias, w_shared_up, w_shared_down, w_expert_up, w_expert_down, ln2_g, ln2_b):
    B, S, D = x.shape
    T = B * S
    xf = x.reshape(T, D)
    qkv, z, ba, sq, sk, sv, ga, gb = _input_projection(xf, _pack_w_in(w_in))

    oa = _deltanet(qkv, z, ba, conv_w, _head_lanes(dn_a_log), _head_lanes(dn_dt_bias),
                   dn_norm_g.reshape(1, DN_HEAD_DIM), B, S)
    ob = _stick_breaking(sq, sk, sv, B, S)

    h, hs, idx, gates, rank, counts = _merge(
        oa, ob, ga, gb, xf, w_proj_a.astype(BF16), w_proj_b.astype(BF16), w_out.astype(BF16), w_router.astype(BF16),
        w_shared_up.astype(BF16), w_shared_down.astype(BF16), ln1_g.reshape(1, D), ln1_b.reshape(1, D),
        router_bias.reshape(1, N_EXPERTS).astype(F32))

    out = _routed_path(h, hs, idx, gates, rank, counts, w_expert_up, w_expert_down, ln2_g.reshape(1, D),
                       ln2_b.reshape(1, D))
    return out.reshape(B, S, D)


def _routed_path(h, hs, idx, gates, rank, counts, w_expert_up, w_expert_down, g2, b2):
    T = hs.shape[0]
    counts = counts.reshape(N_EXPERTS).astype(I32)
    padded = (counts + EXPERT_ROWS - 1) // EXPERT_ROWS * EXPERT_ROWS
    pend = jnp.cumsum(padded)
    pstart = (pend - padded).astype(I32)
    n_blocks = -(-(T * TOP_K + N_EXPERTS * (EXPERT_ROWS - 1)) // EXPERT_ROWS)
    n_used = (pend[-1] // EXPERT_ROWS).astype(I32)
    block_idx = jnp.minimum(jnp.arange(n_blocks, dtype=I32), n_used - 1)
    block_e = jnp.minimum(jnp.sum(pend[None, :] <= (block_idx * EXPERT_ROWS)[:, None], 1), N_EXPERTS - 1).astype(I32)

    prev_e = jnp.concatenate([jnp.full((1,), -1, I32), block_e[:-1]])
    first = (block_e != prev_e).astype(I32)
    slot = ((jnp.cumsum(first) - 1) % 2).astype(I32)
    expert_ids = jnp.arange(N_EXPERTS, dtype=I32)
    later_used = jnp.where(counts > 0, expert_ids, N_EXPERTS)
    next_used = lax.cummin(jnp.concatenate([later_used[1:], jnp.full((1,), N_EXPERTS, I32)]), reverse=True)
    next_used = jnp.where(next_used < N_EXPERTS, next_used, -1)
    next_e = next_used[block_e].astype(I32)

    dest = _dest_rows(idx, rank, pstart.astype(F32).reshape(1, N_EXPERTS))
    dest_flat = dest[:, :TOP_K].reshape(T * TOP_K)
    xs = _dispatch(dest_flat, h, n_blocks * EXPERT_ROWS)
    ys = _experts(block_idx, block_e, first, slot, next_e, n_used.reshape(1), xs, w_expert_up, w_expert_down)
    return _combine(dest_flat, gates, hs, g2, b2, ys)


def kernel(x, w_in, conv_w, dn_a_log, dn_dt_bias, dn_norm_g, w_proj_a, w_proj_b, w_out, ln1_g, ln1_b, w_router,
           router_bias, w_shared_up, w_shared_down, w_expert_up, w_expert_down, ln2_g, ln2_b):
    for layer in range(DEPTH):
        x = _layer(x, w_in[layer], conv_w[layer], dn_a_log[layer], dn_dt_bias[layer], dn_norm_g[layer],
                   w_proj_a[layer], w_proj_b[layer], w_out[layer], ln1_g[layer], ln1_b[layer], w_router[layer],
                   router_bias[layer], w_shared_up[layer], w_shared_down[layer], w_expert_up[layer],
                   w_expert_down[layer], ln2_g[layer], ln2_b[layer])
    return x
```

```python
import functools
import math

import jax
import jax.numpy as jnp
import numpy as np
from jax import lax
from jax.experimental import pallas as pl
from jax.experimental.pallas import tpu as pltpu

F32 = jnp.float32
BF16 = jnp.bfloat16
I32 = jnp.int32

D_MODEL = 1024
CHUNK = 64
DN_HEADS = 4
DN_HEAD_DIM = 128
DN_WIDTH = DN_HEADS * DN_HEAD_DIM
CONV_WIDTH = 4
SB_HEADS = 8
SB_HEAD_DIM = 64
SB_WIDTH = SB_HEADS * SB_HEAD_DIM
N_EXPERTS = 256
TOP_K = 8
N_GROUPS = 8
GROUP_SIZE = N_EXPERTS // N_GROUPS
TOP_GROUPS = 4
EXPERT_FF = 256
SHARED_FF = 256
ROUTED_SCALE = 2.5
EXPERT_BLOCK = 128
DEPTH = 1
DEEPNORM_ALPHA = (2 * DEPTH) ** 0.25
LN_EPS = 1e-5
NORM_EPS = 1e-6

LANES = 128
SUBLANES = 8
VMEM_LIMIT = 56 * 1024 * 1024

_C_QKV = (0, 3 * DN_WIDTH)
_C_Z = (_C_QKV[1], _C_QKV[1] + DN_WIDTH)
_C_BA = (_C_Z[1], _C_Z[1] + LANES)
_C_SQ = (_C_BA[1], _C_BA[1] + SB_WIDTH)
_C_SK = (_C_SQ[1], _C_SQ[1] + SB_WIDTH)
_C_SV = (_C_SK[1], _C_SK[1] + SB_WIDTH)
_C_GA = (_C_SV[1], _C_SV[1] + D_MODEL)
_C_GB = (_C_GA[1], _C_GA[1] + D_MODEL)
PROJ_PACKED = _C_GB[1]


def _params(sem, vmem=VMEM_LIMIT):
    return pltpu.CompilerParams(dimension_semantics=sem, vmem_limit_bytes=vmem)


def _const_spec(shape):
    return pl.BlockSpec(shape, lambda *_: (0,) * len(shape), pipeline_mode=pl.Buffered(1))


def _dot(a, b):
    return jnp.dot(a, b, preferred_element_type=F32)


def _dot_nt(a, b):
    return lax.dot_general(a, b, (((1,), (1,)), ((), ())), preferred_element_type=F32)


def _split_bf16(a):
    hi = a.astype(BF16)
    return hi, (a - hi.astype(F32)).astype(BF16)


U32 = jnp.uint32
HALF_MODEL = D_MODEL // 2
ROW_TILE = HALF_MODEL // LANES
HIGH_HALF_MASK = 0xFFFF0000


def _store_row_tiles(ref, mat):
    n = mat.shape[0]
    lo = lax.bitcast_convert_type(mat[:, :HALF_MODEL].astype(BF16).astype(F32), U32)
    hi = lax.bitcast_convert_type(mat[:, HALF_MODEL:].astype(BF16).astype(F32), U32)
    words = (lo >> 16) | hi
    for s in range(ROW_TILE):
        ref[pl.ds(s, n, stride=ROW_TILE), :] = words[:, s * LANES:(s + 1) * LANES]


def _load_row_tiles(ref, n, lead=()):
    words = [ref[lead + (pl.ds(s, n, stride=ROW_TILE), slice(None))] for s in range(ROW_TILE)]
    lo = [lax.bitcast_convert_type(w << 16, F32) for w in words]
    hi = [lax.bitcast_convert_type(w & U32(HIGH_HALF_MASK), F32) for w in words]
    return jnp.concatenate(lo + hi, axis=1)


def _proj_kernel(x_ref, w_ref, qkv_ref, z_ref, ba_ref, sq_ref, sk_ref, sv_ref, ga_ref, gb_ref):
    xb = x_ref[...].astype(BF16)

    def mm(c):
        return _dot(xb, w_ref[:, c[0]:c[1]])

    qkv_ref[...] = mm(_C_QKV)
    z_ref[...] = mm(_C_Z)
    ba_ref[...] = mm(_C_BA)
    sq_ref[...] = mm(_C_SQ).astype(BF16)
    sk_ref[...] = mm(_C_SK).astype(BF16)
    sv_ref[...] = mm(_C_SV).astype(BF16)
    ga_ref[...] = mm(_C_GA)
    gb_ref[...] = mm(_C_GB)


def _input_projection(xf, w_packed, tm=256):
    T = xf.shape[0]
    widths = [(3 * DN_WIDTH, F32), (DN_WIDTH, F32), (LANES, F32), (SB_WIDTH, BF16), (SB_WIDTH, BF16),
              (SB_WIDTH, BF16), (D_MODEL, F32), (D_MODEL, F32)]
    return pl.pallas_call(
        _proj_kernel,
        out_shape=[jax.ShapeDtypeStruct((T, w), dt) for w, dt in widths],
        grid=(T // tm,),
        in_specs=[pl.BlockSpec((tm, D_MODEL), lambda i: (i, 0)), _const_spec((D_MODEL, PROJ_PACKED))],
        out_specs=[pl.BlockSpec((tm, w), lambda i: (i, 0)) for w, _ in widths],
        compiler_params=_params(("parallel",)),
        name="input_projection",
    )(xf, w_packed)


def _mm3(a, b):
    ah, al = _split_bf16(a)
    bh, bl = _split_bf16(b)
    return _dot(ah, bh) + _dot(ah, bl) + _dot(al, bh)


DN_STEP_CHUNKS = 4


def _deltanet_kernel(qkv_ref, z_ref, ba_ref, cw_ref, alog_ref, dtb_ref, ng_ref, o_ref, win_ref, st_ref):
    n = pl.program_id(1)
    C = CHUNK
    R = DN_STEP_CHUNKS * C
    hd = DN_HEAD_DIM

    @pl.when(n == 0)
    def _():
        win_ref[0:SUBLANES, :] = jnp.zeros((SUBLANES, 3 * DN_WIDTH), F32)
        st_ref[...] = jnp.zeros_like(st_ref)

    win_ref[SUBLANES:SUBLANES + R, :] = qkv_ref[...]
    conv = win_ref[pl.ds(SUBLANES - (CONV_WIDTH - 1), R), :] * cw_ref[0:1, :]
    for tap in range(1, CONV_WIDTH):
        conv = conv + win_ref[pl.ds(SUBLANES - (CONV_WIDTH - 1) + tap, R), :] * cw_ref[tap:tap + 1, :]
    win_ref[0:SUBLANES, :] = win_ref[R:R + SUBLANES, :]
    act_all = _silu(conv)

    bt = ba_ref[...]
    beta_rows = jax.nn.sigmoid(bt)
    g_rows = -jnp.exp(alog_ref[...]) * jax.nn.softplus(bt + dtb_ref[...])
    r_io = lax.broadcasted_iota(I32, (C, C), 0)
    c_io = lax.broadcasted_iota(I32, (C, C), 1)
    incl = r_io >= c_io
    strict = r_io > c_io
    tri = jnp.where(incl, 1.0, 0.0).astype(BF16)
    eye = jnp.where(r_io == c_io, 1.0, 0.0).astype(F32)
    pairs = [(c, h) for c in range(DN_STEP_CHUNKS) for h in range(DN_HEADS)]
    G, eG, eGd, eGl, Gt = [], [], [], [], []
    for c in range(DN_STEP_CHUNKS):
        g_hi, g_lo = _split_bf16(g_rows[c * C:(c + 1) * C])
        G.append(_dot(tri, g_hi) + _dot(tri, g_lo))
    for c in range(DN_STEP_CHUNKS):
        G_last = G[c][C - 1:C, :]
        eG.append(jnp.exp(G[c]))
        eGd.append(jnp.exp(G_last - G[c]))
        eGl.append(jnp.exp(G_last))
        Gt.append(jnp.concatenate([G[c], jnp.zeros((LANES - C, LANES), F32)], axis=0).T)

    q, k, k16, kb, v, beta, decay = {}, {}, {}, {}, {}, {}, {}
    for c, h in pairs:
        act = act_all[c * C:(c + 1) * C]
        qq = act[:, h * hd:(h + 1) * hd]
        kk = act[:, DN_WIDTH + h * hd:DN_WIDTH + (h + 1) * hd]
        v[c, h] = act[:, 2 * DN_WIDTH + h * hd:2 * DN_WIDTH + (h + 1) * hd]
        q[c, h] = qq * lax.rsqrt(jnp.sum(qq * qq, -1, keepdims=True) + NORM_EPS) * (hd ** -0.5)
        k[c, h] = kk * lax.rsqrt(jnp.sum(kk * kk, -1, keepdims=True) + NORM_EPS)
        beta[c, h] = beta_rows[c * C:(c + 1) * C, h:h + 1]
        gl = DN_HEADS + h
        diff = G[c][:, gl:gl + 1] - Gt[c][gl:gl + 1, 0:C]
        decay[c, h] = jnp.where(incl, jnp.exp(jnp.where(incl, diff, 0.0)), 0.0)
        kb[c, h] = k[c, h] * beta[c, h]
        k16[c, h] = k[c, h].astype(BF16)
    M = {p: jnp.where(strict, _dot_nt(kb[p].astype(BF16), k16[p]) * decay[p], 0.0) for p in pairs}
    attn = {p: (_dot_nt(q[p].astype(BF16), k16[p]) * decay[p]).astype(BF16) for p in pairs}
    P = {p: eye - M[p] for p in pairs}
    Mp = M
    for _ in range(5):
        Mp = {p: _mm3(Mp[p], Mp[p]) for p in pairs}
        P = {p: P[p] + _mm3(P[p], Mp[p]) for p in pairs}
    sol = {}
    for c, h in pairs:
        gl = DN_HEADS + h
        rhs = jnp.concatenate([v[c, h] * beta[c, h], kb[c, h] * eG[c][:, gl:gl + 1]], axis=1)
        sol[c, h] = _mm3(P[c, h], rhs)

    heads = range(DN_HEADS)
    S = [st_ref[h] for h in heads]
    for c in range(DN_STEP_CHUNKS):
        wq, vn, o = [], [], []
        for h in heads:
            gl = DN_HEADS + h
            qd = q[c, h] * eG[c][:, gl:gl + 1]
            wq.append(_dot(jnp.concatenate([sol[c, h][:, hd:], qd], axis=0).astype(BF16), S[h].astype(BF16)))
        for h in heads:
            vn.append((sol[c, h][:, :hd] - wq[h][:C]).astype(BF16))
        for h in heads:
            o.append(wq[h][C:] + _dot(attn[c, h], vn[h]))
        for h in heads:
            gl = DN_HEADS + h
            kd = k[c, h] * eGd[c][:, gl:gl + 1]
            S[h] = S[h] * eGl[c][:, gl:gl + 1] + _dot(kd.T.astype(BF16), vn[h])
        outs = []
        for h in heads:
            on = o[h] * lax.rsqrt(jnp.mean(o[h] * o[h], -1, keepdims=True) + NORM_EPS) * ng_ref[...]
            outs.append(on * _silu(z_ref[c * C:(c + 1) * C, h * hd:(h + 1) * hd].astype(F32)))
        o_ref[c * C:(c + 1) * C, :] = jnp.concatenate(outs, axis=1).astype(BF16)
    for h in heads:
        st_ref[h] = S[h]


def _deltanet(qkv, z, ba, conv_w, alog_l, dtb_l, norm_g, B, S):
    rows = DN_STEP_CHUNKS * CHUNK
    N = S // rows
    row = lambda w: pl.BlockSpec((rows, w), lambda b, n: (b * N + n, 0))
    const = lambda s: pl.BlockSpec(s, lambda b, n: (0, 0))
    return pl.pallas_call(
        _deltanet_kernel,
        out_shape=jax.ShapeDtypeStruct((B * S, DN_WIDTH), BF16),
        grid=(B, N),
        in_specs=[row(3 * DN_WIDTH), row(DN_WIDTH), row(LANES), const((CONV_WIDTH, 3 * DN_WIDTH)),
                  const((1, LANES)), const((1, LANES)), const((1, DN_HEAD_DIM))],
        out_specs=row(DN_WIDTH),
        scratch_shapes=[pltpu.VMEM((rows + SUBLANES, 3 * DN_WIDTH), F32),
                        pltpu.VMEM((DN_HEADS, DN_HEAD_DIM, DN_HEAD_DIM), F32)],
        compiler_params=_params(("parallel", "arbitrary")),
        name="gated_deltanet",
    )(qkv, z, ba, conv_w, alog_l, dtb_l, norm_g)


SB_TQ = 512
SB_TK = 256


def _stick_kernel(q_ref, k_ref, v_ref, o_ref):
    i = pl.program_id(2)
    tq, tk = SB_TQ, SB_TK
    ratio = tq // tk
    lane = lax.broadcasted_iota(I32, (tq, LANES), 1)
    u_r = lax.broadcasted_iota(I32, (2 * tk, tk), 0)
    u_c = lax.broadcasted_iota(I32, (2 * tk, tk), 1)
    later = jnp.where((u_r % tk) > u_c, 1.0, 0.0).astype(BF16)
    q = q_ref[...] * (SB_HEAD_DIM ** -0.5)
    zero = jnp.zeros_like(q)
    qs = [jnp.where((lane // SB_HEAD_DIM) == hh, q, zero) for hh in range(2)]

    def blocks(specs, carry, masked):
        kbs, vbs, causal = [], [], []
        for j, r0 in specs:
            off = pl.multiple_of(j * tk, tk)
            kbs.append(k_ref[pl.ds(off, tk), :])
            vbs.append(v_ref[pl.ds(off, tk), :])
            if masked:
                q_pos = lax.broadcasted_iota(I32, (tq - r0, tk), 0) + (r0 + i * tq)
                k_pos = lax.broadcasted_iota(I32, (tq - r0, tk), 1) + j * tk
                causal.append(k_pos < q_pos)
            else:
                causal.append(None)
        units = [(n, hh) for n in range(len(specs)) for hh in range(2)]
        zs = {u: _dot_nt(qs[u[1]][specs[u[0]][1]:], kbs[u[0]]) for u in units}
        log_beta, log_rest = {}, {}
        for u in units:
            lb = jnp.minimum(zs[u], 0.0) - jnp.log(1.0 + jnp.exp(-jnp.abs(zs[u])))
            lr = lb - zs[u]
            if masked:
                lr = jnp.where(causal[u[0]], lr, 0.0)
            log_beta[u], log_rest[u] = lb, lr
        after = {}
        for u in units:
            hi, lo = _split_bf16(log_rest[u])
            after[u] = _dot(jnp.concatenate([hi, lo], axis=1), later)
        carry = list(carry)
        for u in units:
            n, hh = u
            r0 = specs[n][1]
            rest, acc = carry[2 * hh], carry[2 * hh + 1]
            a = jnp.exp(log_beta[u] + after[u] + jnp.concatenate([rest[r0:]] * (tk // LANES), axis=1))
            if masked:
                a = jnp.where(causal[n], a, 0.0)
            acc_new = acc[r0:] + _dot(a.astype(BF16), vbs[n])
            rest_new = rest[r0:] + jnp.broadcast_to(after[u][:, 0:1] + log_rest[u][:, 0:1], (tq - r0, LANES))
            carry[2 * hh + 1] = acc_new if r0 == 0 else jnp.concatenate([acc[:r0], acc_new], axis=0)
            carry[2 * hh] = rest_new if r0 == 0 else jnp.concatenate([rest[:r0], rest_new], axis=0)
        return tuple(carry)

    carry = (jnp.zeros((tq, LANES), F32), jnp.zeros((tq, LANES), F32)) * 2
    carry = blocks([(i * ratio + d, d * tk) for d in range(ratio - 1, -1, -1)], carry, True)
    carry = lax.fori_loop(
        0, i, lambda t, c: blocks([((i - t) * ratio - 1 - d, 0) for d in range(ratio)], c, False), carry)
    o_ref[...] = jnp.where(lane < SB_HEAD_DIM, carry[1], carry[3]).astype(BF16)


def _stick_breaking(sq, sk, sv, B, S):
    nq = S // SB_TQ
    pairs = SB_WIDTH // LANES
    q_spec = pl.BlockSpec((SB_TQ, LANES), lambda b, p, i: (b * nq + i, p))
    kv_spec = pl.BlockSpec((S, LANES), lambda b, p, i: (b, p))
    return pl.pallas_call(
        _stick_kernel,
        out_shape=jax.ShapeDtypeStruct((B * S, SB_WIDTH), BF16),
        grid=(B, pairs, nq),
        in_specs=[q_spec, kv_spec, kv_spec],
        out_specs=q_spec,
        compiler_params=_params(("parallel", "parallel", "arbitrary")),
        name="stick_breaking_attention",
    )(sq, sk, sv)


def _layer_norm(v, g, b):
    mu = jnp.mean(v, -1, keepdims=True)
    c = v - mu
    var = jnp.mean(c * c, -1, keepdims=True)
    return c * lax.rsqrt(var + LN_EPS) * g + b


def _silu(v):
    return v * jax.nn.sigmoid(v)


def _merge_kernel(oa_ref, ob_ref, ga_ref, gb_ref, x_ref, wpa_ref, wpb_ref, wout_ref, wr_ref, wsu_ref, wsd_ref,
                  g1_ref, b1_ref, rbias_ref, h_ref, hs_ref, idx_ref, gate_ref, rank_ref, cnt_ref, run_ref):
    ya = _dot(oa_ref[...], wpa_ref[...])
    yb = _dot(ob_ref[...], wpb_ref[...])
    merged = jax.nn.sigmoid(ga_ref[...].astype(F32)) * ya + jax.nn.sigmoid(gb_ref[...].astype(F32)) * yb
    pre = DEEPNORM_ALPHA * x_ref[...] + _dot(merged.astype(BF16), wout_ref[...])
    h = _layer_norm(pre, g1_ref[...], b1_ref[...])
    _store_row_tiles(h_ref, h)
    hb = h.astype(BF16)
    scores = jax.nn.sigmoid(_dot(hb, wr_ref[...]))
    up = _dot(hb, wsu_ref[...])
    act = _silu(up[:, :SHARED_FF]) * up[:, SHARED_FF:]
    hs_ref[...] = DEEPNORM_ALPHA * h + _dot(act.astype(BF16), wsd_ref[...])
    _route_tile(scores, rbias_ref, idx_ref, gate_ref, rank_ref, cnt_ref, run_ref)


def _merge(oa, ob, ga, gb, xf, wpa, wpb, wout, wr, wsu, wsd, g1, b1, rbias, tm=256):
    T = xf.shape[0]
    row = lambda w: pl.BlockSpec((tm, w), lambda i: (i, 0))
    return pl.pallas_call(
        _merge_kernel,
        out_shape=[jax.ShapeDtypeStruct((T * ROW_TILE, LANES), U32), jax.ShapeDtypeStruct((T, D_MODEL), F32),
                   jax.ShapeDtypeStruct((T, LANES), I32), jax.ShapeDtypeStruct((T, LANES), F32),
                   jax.ShapeDtypeStruct((T, LANES), I32), jax.ShapeDtypeStruct((1, N_EXPERTS), F32)],
        grid=(T // tm,),
        in_specs=[row(DN_WIDTH), row(SB_WIDTH), row(D_MODEL), row(D_MODEL), row(D_MODEL),
                  _const_spec(wpa.shape), _const_spec(wpb.shape), _const_spec(wout.shape), _const_spec(wr.shape),
                  _const_spec(wsu.shape), _const_spec(wsd.shape), _const_spec(g1.shape), _const_spec(b1.shape),
                  _const_spec(rbias.shape)],
        out_specs=[pl.BlockSpec((tm * ROW_TILE, LANES), lambda i: (i, 0)), row(D_MODEL), row(LANES), row(LANES),
                   row(LANES), pl.BlockSpec((1, N_EXPERTS), lambda i: (0, 0))],
        scratch_shapes=[pltpu.VMEM((1, N_EXPERTS), F32)],
        compiler_params=_params(("arbitrary",)),
        name="merge_norm_shared_route",
    )(oa, ob, ga, gb, xf, wpa, wpb, wout, wr, wsu, wsd, g1, b1, rbias)


def _route_tile(scores, bias_ref, idx_ref, gate_ref, rank_ref, cnt_ref, run_ref):
    i = pl.program_id(0)
    tm = scores.shape[0]

    @pl.when(i == 0)
    def _():
        run_ref[...] = jnp.zeros_like(run_ref)

    sel = scores + bias_ref[...]
    lane_i = lax.broadcasted_iota(I32, (tm, N_EXPERTS), 1)
    grp = lane_i // GROUP_SIZE
    lane = lane_i.astype(F32)
    neg = F32(-jnp.inf)

    def first_argmax(v):
        m = jnp.max(v, -1, keepdims=True)
        ix = jnp.min(jnp.where(v == m, lane, float(N_EXPERTS)), -1, keepdims=True)
        return m, ix

    gscore = []
    for g in range(N_GROUPS):
        gv = jnp.where(grp == g, sel, neg)
        m1, i1 = first_argmax(gv)
        m2 = jnp.max(jnp.where(lane == i1, neg, gv), -1, keepdims=True)
        gscore.append(m1 + m2)
    emask = jnp.zeros((tm, N_EXPERTS), jnp.bool_)
    for g in range(N_GROUPS):
        beaten = jnp.zeros((tm, 1), I32)
        for o in range(N_GROUPS):
            if o == g:
                continue
            wins = (gscore[o] > gscore[g]) | ((gscore[o] == gscore[g]) & (o < g))
            beaten = beaten + wins.astype(I32)
        emask = emask | ((grp == g) & (beaten < TOP_GROUPS))
    masked = jnp.where(emask, sel, neg)

    picked = jnp.zeros((tm, N_EXPERTS), jnp.bool_)
    idx_cols, score_cols, hots = [], [], []
    for _ in range(TOP_K):
        _, ix = first_argmax(masked)
        hot = lane == ix
        idx_cols.append(ix)
        score_cols.append(jnp.sum(jnp.where(hot, scores, 0.0), -1, keepdims=True))
        hots.append(hot)
        masked = jnp.where(hot, neg, masked)
        picked = picked | hot
    chosen = jnp.where(picked, 1.0, 0.0).astype(BF16)
    r_io = lax.broadcasted_iota(I32, (tm, tm), 0)
    c_io = lax.broadcasted_iota(I32, (tm, tm), 1)
    strict_lower = jnp.where(c_io < r_io, 1.0, 0.0).astype(BF16)
    before = _dot(strict_lower, chosen) + run_ref[...]
    run_ref[...] = run_ref[...] + jnp.sum(chosen.astype(F32), 0, keepdims=True)
    cnt_ref[...] = run_ref[...]

    total = score_cols[0]
    for s in score_cols[1:]:
        total = total + s
    lane_o = lax.broadcasted_iota(I32, (tm, LANES), 1)
    idx_o = jnp.zeros((tm, LANES), F32)
    gate_o = jnp.zeros((tm, LANES), F32)
    rank_o = jnp.zeros((tm, LANES), F32)
    for k in range(TOP_K):
        rk = jnp.sum(jnp.where(hots[k], before, 0.0), -1, keepdims=True)
        idx_o = jnp.where(lane_o == k, idx_cols[k], idx_o)
        gate_o = jnp.where(lane_o == k, score_cols[k] / total * ROUTED_SCALE, gate_o)
        rank_o = jnp.where(lane_o == k, rk, rank_o)
    idx_ref[...] = idx_o.astype(I32)
    gate_ref[...] = gate_o
    rank_ref[...] = rank_o.astype(I32)


def _dest_kernel(idx_ref, rank_ref, pstart_ref, dest_ref):
    tm = idx_ref.shape[0]
    idx = idx_ref[...]
    lane = lax.broadcasted_iota(I32, (tm, N_EXPERTS), 1)
    lane_o = lax.broadcasted_iota(I32, (tm, LANES), 1)
    start = jnp.zeros((tm, LANES), F32)
    for k in range(TOP_K):
        sk = jnp.sum(jnp.where(lane == idx[:, k:k + 1], pstart_ref[...], 0.0), -1, keepdims=True)
        start = jnp.where(lane_o == k, sk, start)
    dest_ref[...] = start.astype(I32) + rank_ref[...]


def _dest_rows(idx, rank, pstart_f, tm=512):
    T = idx.shape[0]
    row = pl.BlockSpec((tm, LANES), lambda i: (i, 0))
    return pl.pallas_call(
        _dest_kernel,
        out_shape=jax.ShapeDtypeStruct((T, LANES), I32),
        grid=(T // tm,),
        in_specs=[row, row, pl.BlockSpec((1, N_EXPERTS), lambda i: (0, 0))],
        out_specs=row,
        compiler_params=_params(("arbitrary",)),
        name="dest_rows",
    )(idx, rank, pstart_f)


DISPATCH_TOKENS = 512
COMBINE_TOKENS = 128
EXPERT_ROWS = 256


def _dispatch_kernel(dest_ref, h_ref, xs_hbm, sem):
    def issue(t, carry):
        src = pl.multiple_of(t * ROW_TILE, ROW_TILE)
        for k in range(TOP_K):
            dst = pl.multiple_of(dest_ref[t * TOP_K + k] * ROW_TILE, ROW_TILE)
            pltpu.make_async_copy(h_ref.at[pl.ds(src, ROW_TILE)], xs_hbm.at[pl.ds(dst, ROW_TILE)],
                                  sem).start(priority=k % 2)
        return carry

    lax.fori_loop(0, DISPATCH_TOKENS, issue, 0)
    for _ in range(TOP_K):
        pltpu.make_async_copy(h_ref, xs_hbm.at[pl.ds(0, DISPATCH_TOKENS * ROW_TILE)], sem).wait()


def _dispatch(dest_flat, h_rt, n_rows):
    T = h_rt.shape[0] // ROW_TILE
    tm = DISPATCH_TOKENS
    return pl.pallas_call(
        _dispatch_kernel,
        out_shape=jax.ShapeDtypeStruct((n_rows * ROW_TILE, LANES), U32),
        grid=(T // tm,),
        in_specs=[pl.BlockSpec((tm * TOP_K,), lambda i: (i,), memory_space=pltpu.SMEM),
                  pl.BlockSpec((tm * ROW_TILE, LANES), lambda i: (i, 0))],
        out_specs=pl.BlockSpec(memory_space=pl.ANY),
        scratch_shapes=[pltpu.SemaphoreType.DMA(())],
        compiler_params=pltpu.CompilerParams(dimension_semantics=("arbitrary",), has_side_effects=True,
                                             vmem_limit_bytes=VMEM_LIMIT),
        name="dispatch_rows",
    )(dest_flat, h_rt)


def _expert_kernel(blk_ref, be_ref, first_ref, slot_ref, nexte_ref, nu_ref, xs_ref, wu_hbm, wd_hbm, ys_ref,
                   wu32_ref, wd32_ref, wu16_ref, wd16_ref, sems):
    b = pl.program_id(0)
    used = b < nu_ref[0]

    def weight_copies(e, s):
        return (pltpu.make_async_copy(wu_hbm.at[e], wu32_ref.at[s], sems.at[0, s]),
                pltpu.make_async_copy(wd_hbm.at[e], wd32_ref.at[s], sems.at[1, s]))

    @pl.when(b == 0)
    def _():
        for c in weight_copies(be_ref[0], 0):
            c.start()

    @pl.when(used & (first_ref[b] == 1))
    def _():
        s = slot_ref[b]
        for c in weight_copies(be_ref[b], s):
            c.wait()
        wu16_ref[...] = wu32_ref[s].astype(BF16)
        wd16_ref[...] = wd32_ref[s].astype(BF16)

        @pl.when(nexte_ref[b] >= 0)
        def _():
            for c in weight_copies(nexte_ref[b], 1 - s):
                c.start()

    @pl.when(used)
    def _():
        xb = _load_row_tiles(xs_ref, EXPERT_ROWS).astype(BF16)
        up = _dot(xb, wu16_ref[...])
        act = _silu(up[:, :EXPERT_FF]) * up[:, EXPERT_FF:]
        _store_row_tiles(ys_ref, _dot(act.astype(BF16), wd16_ref[...]))


def _experts(block_idx, block_e, first, slot, next_e, n_used, xs, w_up, w_down):
    n_blocks = xs.shape[0] // (EXPERT_ROWS * ROW_TILE)
    rows = pl.BlockSpec((EXPERT_ROWS * ROW_TILE, LANES), lambda b, bi, *_: (bi[b], 0))
    grid_spec = pltpu.PrefetchScalarGridSpec(
        num_scalar_prefetch=6,
        grid=(n_blocks,),
        in_specs=[rows, pl.BlockSpec(memory_space=pl.ANY), pl.BlockSpec(memory_space=pl.ANY)],
        out_specs=rows,
        scratch_shapes=[pltpu.VMEM((2, D_MODEL, 2 * EXPERT_FF), F32), pltpu.VMEM((2, EXPERT_FF, D_MODEL), F32),
                        pltpu.VMEM((D_MODEL, 2 * EXPERT_FF), BF16), pltpu.VMEM((EXPERT_FF, D_MODEL), BF16),
                        pltpu.SemaphoreType.DMA((2, 2))],
    )
    return pl.pallas_call(
        _expert_kernel,
        out_shape=jax.ShapeDtypeStruct(xs.shape, U32),
        grid_spec=grid_spec,
        compiler_params=_params(("arbitrary",)),
        name="routed_experts",
    )(block_idx, block_e, first, slot, next_e, n_used, xs, w_up, w_down)


def _combine_kernel(dest_ref, destn_ref, gate_ref, hs_ref, g2_ref, b2_ref, ys_hbm, out_ref, ybuf, sems):
    i = pl.program_id(0)
    n = pl.num_programs(0)
    slot = i % 2

    def gather(dr, s):
        def issue(t, carry):
            dst = pl.multiple_of(t * ROW_TILE, ROW_TILE)
            for k in range(TOP_K):
                src = pl.multiple_of(dr[t * TOP_K + k] * ROW_TILE, ROW_TILE)
                pltpu.make_async_copy(ys_hbm.at[pl.ds(src, ROW_TILE)], ybuf.at[s, k, pl.ds(dst, ROW_TILE)],
                                      sems.at[s]).start(priority=k % 2)
            return carry

        lax.fori_loop(0, COMBINE_TOKENS, issue, 0)

    @pl.when(i == 0)
    def _():
        gather(dest_ref, 0)

    @pl.when(i + 1 < n)
    def _():
        gather(destn_ref, 1 - slot)

    for k in range(TOP_K):
        pltpu.make_async_copy(ys_hbm.at[pl.ds(0, COMBINE_TOKENS * ROW_TILE)], ybuf.at[slot, k],
                              sems.at[slot]).wait()

    gates = gate_ref[...]
    acc = hs_ref[...]
    for k in range(TOP_K):
        acc = acc + gates[:, k:k + 1] * _load_row_tiles(ybuf, COMBINE_TOKENS, (slot, k))
    out_ref[...] = _layer_norm(acc, g2_ref[...], b2_ref[...])


def _combine(dest_flat, gates, hs, g2, b2, ys):
    T = hs.shape[0]
    tm = COMBINE_TOKENS
    n = T // tm
    cur = pl.BlockSpec((tm * TOP_K,), lambda i: (i,), memory_space=pltpu.SMEM)
    nxt = pl.BlockSpec((tm * TOP_K,), lambda i: (jnp.minimum(i + 1, n - 1),), memory_space=pltpu.SMEM)
    return pl.pallas_call(
        _combine_kernel,
        out_shape=jax.ShapeDtypeStruct((T, D_MODEL), F32),
        grid=(n,),
        in_specs=[cur, nxt, pl.BlockSpec((tm, LANES), lambda i: (i, 0)), pl.BlockSpec((tm, D_MODEL), lambda i: (i, 0)),
                  pl.BlockSpec((1, D_MODEL), lambda i: (0, 0)), pl.BlockSpec((1, D_MODEL), lambda i: (0, 0)),
                  pl.BlockSpec(memory_space=pl.ANY)],
        out_specs=pl.BlockSpec((tm, D_MODEL), lambda i: (i, 0)),
        scratch_shapes=[pltpu.VMEM((2, TOP_K, tm * ROW_TILE, LANES), U32), pltpu.SemaphoreType.DMA((2,))],
        compiler_params=_params(("arbitrary",)),
        name="combine_norm",
    )(dest_flat, dest_flat, gates, hs, g2, b2, ys)


def _pack_w_in(w_in):
    c1 = 4 * DN_WIDTH
    c2 = c1 + 2 * DN_HEADS
    ba = jnp.pad(w_in[:, c1:c2], ((0, 0), (0, LANES - 2 * DN_HEADS)))
    return jnp.concatenate([w_in[:, :c1], ba, w_in[:, c2:]], axis=1).astype(BF16)


def _head_lanes(v):
    return jnp.pad(v.astype(F32), (DN_HEADS, LANES - 2 * DN_HEADS)).reshape(1, LANES)


def _layer(x, w_in, conv_w, dn_a_log, dn_dt_bias, dn_norm_g, w_proj_a, w_proj_b, w_out, ln1_g, ln1_b, w_router,
           router_bias, w_shared_up, w_shared_down, w_expert_up, w_expert_down, ln2_g, ln2_b):
    B, S, D = x.shape
    T = B * S
    xf = x.reshape(T, D)
    qkv, z, ba, sq, sk, sv, ga, gb = _input_projection(xf, _pack_w_in(w_in))

    oa = _deltanet(qkv, z, ba, conv_w, _head_lanes(dn_a_log), _head_lanes(dn_dt_bias),
                   dn_norm_g.reshape(1, DN_HEAD_DIM), B, S)
    ob = _stick_breaking(sq, sk, sv, B, S)

    h, hs, idx, gates, rank, counts = _merge(
        oa, ob, ga, gb, xf, w_proj_a.astype(BF16), w_proj_b.astype(BF16), w_out.astype(BF16), w_router.astype(BF16),
        w_shared_up.astype(BF16), w_shared_down.astype(BF16), ln1_g.reshape(1, D), ln1_b.reshape(1, D),
        router_bias.reshape(1, N_EXPERTS).astype(F32))

    out = _routed_path(h, hs, idx, gates, rank, counts, w_expert_up, w_expert_down, ln2_g.reshape(1, D),
                       ln2_b.reshape(1, D))
    return out.reshape(B, S, D)


def _routed_path(h, hs, idx, gates, rank, counts, w_expert_up, w_expert_down, g2, b2):
    T = hs.shape[0]
    counts = counts.reshape(N_EXPERTS).astype(I32)
    padded = (counts + EXPERT_ROWS - 1) // EXPERT_ROWS * EXPERT_ROWS
    pend = jnp.cumsum(padded)
    pstart = (pend - padded).astype(I32)
    n_blocks = -(-(T * TOP_K + N_EXPERTS * (EXPERT_ROWS - 1)) // EXPERT_ROWS)
    n_used = (pend[-1] // EXPERT_ROWS).astype(I32)
    block_idx = jnp.minimum(jnp.arange(n_blocks, dtype=I32), n_used - 1)
    block_e = jnp.minimum(jnp.sum(pend[None, :] <= (block_idx * EXPERT_ROWS)[:, None], 1), N_EXPERTS - 1).astype(I32)

    prev_e = jnp.concatenate([jnp.full((1,), -1, I32), block_e[:-1]])
    first = (block_e != prev_e).astype(I32)
    slot = ((jnp.cumsum(first) - 1) % 2).astype(I32)
    expert_ids = jnp.arange(N_EXPERTS, dtype=I32)
    later_used = jnp.where(counts > 0, expert_ids, N_EXPERTS)
    next_used = lax.cummin(jnp.concatenate([later_used[1:], jnp.full((1,), N_EXPERTS, I32)]), reverse=True)
    next_used = jnp.where(next_used < N_EXPERTS, next_used, -1)
    next_e = next_used[block_e].astype(I32)

    dest = _dest_rows(idx, rank, pstart.astype(F32).reshape(1, N_EXPERTS))
    dest_flat = dest[:, :TOP_K].reshape(T * TOP_K)
    xs = _dispatch(dest_flat, h, n_blocks * EXPERT_ROWS)
    ys = _experts(block_idx, block_e, first, slot, next_e, n_used.reshape(1), xs, w_expert_up, w_expert_down)
    return _combine(dest_flat, gates, hs, g2, b2, ys)


def kernel(x, w_in, conv_w, dn_a_log, dn_dt_bias, dn_norm_g, w_proj_a, w_proj_b, w_out, ln1_g, ln1_b, w_router,
           router_bias, w_shared_up, w_shared_down, w_expert_up, w_expert_down, ln2_g, ln2_b):
    for layer in range(DEPTH):
        x = _layer(x, w_in[layer], conv_w[layer], dn_a_log[layer], dn_dt_bias[layer], dn_norm_g[layer],
                   w_proj_a[layer], w_proj_b[layer], w_out[layer], ln1_g[layer], ln1_b[layer], w_router[layer],
                   router_bias[layer], w_shared_up[layer], w_shared_down[layer], w_expert_up[layer],
                   w_expert_down[layer], ln2_g[layer], ln2_b[layer])
    return x
```

```python
import functools
import math

import jax
import jax.numpy as jnp
import numpy as np
from jax import lax
from jax.experimental import pallas as pl
from jax.experimental.pallas import tpu as pltpu

F32 = jnp.float32
BF16 = jnp.bfloat16
I32 = jnp.int32

D_MODEL = 1024
CHUNK = 64
DN_HEADS = 4
DN_HEAD_DIM = 128
DN_WIDTH = DN_HEADS * DN_HEAD_DIM
CONV_WIDTH = 4
SB_HEADS = 8
SB_HEAD_DIM = 64
SB_WIDTH = SB_HEADS * SB_HEAD_DIM
N_EXPERTS = 256
TOP_K = 8
N_GROUPS = 8
GROUP_SIZE = N_EXPERTS // N_GROUPS
TOP_GROUPS = 4
EXPERT_FF = 256
SHARED_FF = 256
ROUTED_SCALE = 2.5
EXPERT_BLOCK = 128
DEPTH = 1
DEEPNORM_ALPHA = (2 * DEPTH) ** 0.25
LN_EPS = 1e-5
NORM_EPS = 1e-6

LANES = 128
SUBLANES = 8
VMEM_LIMIT = 56 * 1024 * 1024

_C_QKV = (0, 3 * DN_WIDTH)
_C_Z = (_C_QKV[1], _C_QKV[1] + DN_WIDTH)
_C_BA = (_C_Z[1], _C_Z[1] + LANES)
_C_SQ = (_C_BA[1], _C_BA[1] + SB_WIDTH)
_C_SK = (_C_SQ[1], _C_SQ[1] + SB_WIDTH)
_C_SV = (_C_SK[1], _C_SK[1] + SB_WIDTH)
_C_GA = (_C_SV[1], _C_SV[1] + D_MODEL)
_C_GB = (_C_GA[1], _C_GA[1] + D_MODEL)
PROJ_PACKED = _C_GB[1]


def _params(sem, vmem=VMEM_LIMIT):
    return pltpu.CompilerParams(dimension_semantics=sem, vmem_limit_bytes=vmem)


def _const_spec(shape):
    return pl.BlockSpec(shape, lambda *_: (0,) * len(shape), pipeline_mode=pl.Buffered(1))


def _dot(a, b):
    return jnp.dot(a, b, preferred_element_type=F32)


def _dot_nt(a, b):
    return lax.dot_general(a, b, (((1,), (1,)), ((), ())), preferred_element_type=F32)


def _split_bf16(a):
    hi = a.astype(BF16)
    return hi, (a - hi.astype(F32)).astype(BF16)


U32 = jnp.uint32
HALF_MODEL = D_MODEL // 2
ROW_TILE = HALF_MODEL // LANES
HIGH_HALF_MASK = 0xFFFF0000


def _store_row_tiles(ref, mat):
    n = mat.shape[0]
    lo = lax.bitcast_convert_type(mat[:, :HALF_MODEL].astype(BF16).astype(F32), U32)
    hi = lax.bitcast_convert_type(mat[:, HALF_MODEL:].astype(BF16).astype(F32), U32)
    words = (lo >> 16) | hi
    for s in range(ROW_TILE):
        ref[pl.ds(s, n, stride=ROW_TILE), :] = words[:, s * LANES:(s + 1) * LANES]


def _load_row_tiles(ref, n, lead=()):
    words = [ref[lead + (pl.ds(s, n, stride=ROW_TILE), slice(None))] for s in range(ROW_TILE)]
    lo = [lax.bitcast_convert_type(w << 16, F32) for w in words]
    hi = [lax.bitcast_convert_type(w & U32(HIGH_HALF_MASK), F32) for w in words]
    return jnp.concatenate(lo + hi, axis=1)


def _proj_kernel(x_ref, w_ref, qkv_ref, z_ref, ba_ref, sq_ref, sk_ref, sv_ref, ga_ref, gb_ref):
    xb = x_ref[...].astype(BF16)

    def mm(c):
        return _dot(xb, w_ref[:, c[0]:c[1]])

    qkv_ref[...] = mm(_C_QKV)
    z_ref[...] = mm(_C_Z)
    ba_ref[...] = mm(_C_BA)
    sq_ref[...] = mm(_C_SQ).astype(BF16)
    sk_ref[...] = mm(_C_SK).astype(BF16)
    sv_ref[...] = mm(_C_SV).astype(BF16)
    ga_ref[...] = mm(_C_GA)
    gb_ref[...] = mm(_C_GB)


def _input_projection(xf, w_packed, tm=256):
    T = xf.shape[0]
    widths = [(3 * DN_WIDTH, F32), (DN_WIDTH, F32), (LANES, F32), (SB_WIDTH, BF16), (SB_WIDTH, BF16),
              (SB_WIDTH, BF16), (D_MODEL, F32), (D_MODEL, F32)]
    return pl.pallas_call(
        _proj_kernel,
        out_shape=[jax.ShapeDtypeStruct((T, w), dt) for w, dt in widths],
        grid=(T // tm,),
        in_specs=[pl.BlockSpec((tm, D_MODEL), lambda i: (i, 0)), _const_spec((D_MODEL, PROJ_PACKED))],
        out_specs=[pl.BlockSpec((tm, w), lambda i: (i, 0)) for w, _ in widths],
        compiler_params=_params(("parallel",)),
        name="input_projection",
    )(xf, w_packed)


DN_STEP_CHUNKS = 4


def _deltanet_kernel(qkv_ref, z_ref, ba_ref, cw_ref, alog_ref, dtb_ref, ng_ref, o_ref, win_ref, st_ref):
    n = pl.program_id(1)
    C = CHUNK
    R = DN_STEP_CHUNKS * C
    hd = DN_HEAD_DIM

    @pl.when(n == 0)
    def _():
        win_ref[0:SUBLANES, :] = jnp.zeros((SUBLANES, 3 * DN_WIDTH), F32)
        st_ref[...] = jnp.zeros_like(st_ref)

    win_ref[SUBLANES:SUBLANES + R, :] = qkv_ref[...]
    conv = win_ref[pl.ds(SUBLANES - (CONV_WIDTH - 1), R), :] * cw_ref[0:1, :]
    for tap in range(1, CONV_WIDTH):
        conv = conv + win_ref[pl.ds(SUBLANES - (CONV_WIDTH - 1) + tap, R), :] * cw_ref[tap:tap + 1, :]
    win_ref[0:SUBLANES, :] = win_ref[R:R + SUBLANES, :]
    act_all = _silu(conv)

    bt = ba_ref[...]
    beta_rows = jax.nn.sigmoid(bt)
    g_rows = -jnp.exp(alog_ref[...]) * jax.nn.softplus(bt + dtb_ref[...])
    r_io = lax.broadcasted_iota(I32, (C, C), 0)
    c_io = lax.broadcasted_iota(I32, (C, C), 1)
    incl = r_io >= c_io
    strict = r_io > c_io
    tri = jnp.where(incl, 1.0, 0.0).astype(BF16)
    eye = jnp.where(r_io == c_io, 1.0, 0.0).astype(F32)
    pairs = [(c, h) for c in range(DN_STEP_CHUNKS) for h in range(DN_HEADS)]
    G, eG, eGd, eGl, Gt = [], [], [], [], []
    for c in range(DN_STEP_CHUNKS):
        g_hi, g_lo = _split_bf16(g_rows[c * C:(c + 1) * C])
        G.append(_dot(tri, g_hi) + _dot(tri, g_lo))
    for c in range(DN_STEP_CHUNKS):
        G_last = G[c][C - 1:C, :]
        eG.append(jnp.exp(G[c]))
        eGd.append(jnp.exp(G_last - G[c]))
        eGl.append(jnp.exp(G_last))
        Gt.append(jnp.concatenate([G[c], jnp.zeros((LANES - C, LANES), F32)], axis=0).T)

    q, k, k16, kb, v, beta, decay = {}, {}, {}, {}, {}, {}, {}
    for c, h in pairs:
        act = act_all[c * C:(c + 1) * C]
        qq = act[:, h * hd:(h + 1) * hd]
        kk = act[:, DN_WIDTH + h * hd:DN_WIDTH + (h + 1) * hd]
        v[c, h] = act[:, 2 * DN_WIDTH + h * hd:2 * DN_WIDTH + (h + 1) * hd]
        q[c, h] = qq * lax.rsqrt(jnp.sum(qq * qq, -1, keepdims=True) + NORM_EPS) * (hd ** -0.5)
        k[c, h] = kk * lax.rsqrt(jnp.sum(kk * kk, -1, keepdims=True) + NORM_EPS)
        beta[c, h] = beta_rows[c * C:(c + 1) * C, h:h + 1]
        gl = DN_HEADS + h
        diff = G[c][:, gl:gl + 1] - Gt[c][gl:gl + 1, 0:C]
        decay[c, h] = jnp.where(incl, jnp.exp(jnp.where(incl, diff, 0.0)), 0.0)
        kb[c, h] = k[c, h] * beta[c, h]
        k16[c, h] = k[c, h].astype(BF16)
    M = {p: jnp.where(strict, _dot_nt(kb[p].astype(BF16), k16[p]) * decay[p], 0.0) for p in pairs}
    attn = {p: (_dot_nt(q[p].astype(BF16), k16[p]) * decay[p]).astype(BF16) for p in pairs}
    P = {p: eye - M[p] for p in pairs}
    Mp16 = {p: M[p].astype(BF16) for p in pairs}
    for _ in range(5):
        Mp16 = {p: _dot(Mp16[p], Mp16[p]).astype(BF16) for p in pairs}
        P = {p: P[p] + _dot(P[p].astype(BF16), Mp16[p]) for p in pairs}
    sol = {}
    for c, h in pairs:
        gl = DN_HEADS + h
        rhs = jnp.concatenate([v[c, h] * beta[c, h], kb[c, h] * eG[c][:, gl:gl + 1]], axis=1)
        sol[c, h] = _dot(P[c, h].astype(BF16), rhs.astype(BF16))

    heads = range(DN_HEADS)
    S = [st_ref[h] for h in heads]
    for c in range(DN_STEP_CHUNKS):
        wq, vn, o = [], [], []
        for h in heads:
            gl = DN_HEADS + h
            qd = q[c, h] * eG[c][:, gl:gl + 1]
            wq.append(_dot(jnp.concatenate([sol[c, h][:, hd:], qd], axis=0).astype(BF16), S[h].astype(BF16)))
        for h in heads:
            vn.append((sol[c, h][:, :hd] - wq[h][:C]).astype(BF16))
        for h in heads:
            o.append(wq[h][C:] + _dot(attn[c, h], vn[h]))
        for h in heads:
            gl = DN_HEADS + h
            kd = k[c, h] * eGd[c][:, gl:gl + 1]
            S[h] = S[h] * eGl[c][:, gl:gl + 1] + _dot(kd.T.astype(BF16), vn[h])
        outs = []
        for h in heads:
            on = o[h] * lax.rsqrt(jnp.mean(o[h] * o[h], -1, keepdims=True) + NORM_EPS) * ng_ref[...]
            outs.append(on * _silu(z_ref[c * C:(c + 1) * C, h * hd:(h + 1) * hd].astype(F32)))
        o_ref[c * C:(c + 1) * C, :] = jnp.concatenate(outs, axis=1).astype(BF16)
    for h in heads:
        st_ref[h] = S[h]


def _deltanet(qkv, z, ba, conv_w, alog_l, dtb_l, norm_g, B, S):
    rows = DN_STEP_CHUNKS * CHUNK
    N = S // rows
    row = lambda w: pl.BlockSpec((rows, w), lambda b, n: (b * N + n, 0))
    const = lambda s: pl.BlockSpec(s, lambda b, n: (0, 0))
    return pl.pallas_call(
        _deltanet_kernel,
        out_shape=jax.ShapeDtypeStruct((B * S, DN_WIDTH), BF16),
        grid=(B, N),
        in_specs=[row(3 * DN_WIDTH), row(DN_WIDTH), row(LANES), const((CONV_WIDTH, 3 * DN_WIDTH)),
                  const((1, LANES)), const((1, LANES)), const((1, DN_HEAD_DIM))],
        out_specs=row(DN_WIDTH),
        scratch_shapes=[pltpu.VMEM((rows + SUBLANES, 3 * DN_WIDTH), F32),
                        pltpu.VMEM((DN_HEADS, DN_HEAD_DIM, DN_HEAD_DIM), F32)],
        compiler_params=_params(("parallel", "arbitrary")),
        name="gated_deltanet",
    )(qkv, z, ba, conv_w, alog_l, dtb_l, norm_g)


SB_TQ = 512
SB_TK = 256


def _stick_kernel(q_ref, k_ref, v_ref, o_ref):
    i = pl.program_id(2)
    tq, tk = SB_TQ, SB_TK
    ratio = tq // tk
    lane = lax.broadcasted_iota(I32, (tq, LANES), 1)
    u_r = lax.broadcasted_iota(I32, (tk, tk), 0)
    u_c = lax.broadcasted_iota(I32, (tk, tk), 1)
    later = jnp.where(u_r > u_c, 1.0, 0.0).astype(BF16)
    q = q_ref[...] * (SB_HEAD_DIM ** -0.5)
    zero = jnp.zeros_like(q)
    qs = [jnp.where((lane // SB_HEAD_DIM) == hh, q, zero) for hh in range(2)]

    def blocks(specs, carry, masked):
        kbs, vbs, causal = [], [], []
        for j, r0 in specs:
            off = pl.multiple_of(j * tk, tk)
            kbs.append(k_ref[pl.ds(off, tk), :])
            vbs.append(v_ref[pl.ds(off, tk), :])
            if masked:
                q_pos = lax.broadcasted_iota(I32, (tq - r0, tk), 0) + (r0 + i * tq)
                k_pos = lax.broadcasted_iota(I32, (tq - r0, tk), 1) + j * tk
                causal.append(k_pos < q_pos)
            else:
                causal.append(None)
        units = [(n, hh) for n in range(len(specs)) for hh in range(2)]
        zs = {u: _dot_nt(qs[u[1]][specs[u[0]][1]:], kbs[u[0]]) for u in units}
        log_beta, log_rest = {}, {}
        for u in units:
            lb = jnp.minimum(zs[u], 0.0) - jnp.log(1.0 + jnp.exp(-jnp.abs(zs[u])))
            lr = lb - zs[u]
            if masked:
                lr = jnp.where(causal[u[0]], lr, 0.0)
            log_beta[u], log_rest[u] = lb, lr
        after = {}
        for u in units:
            after[u] = _dot(log_rest[u].astype(BF16), later)
        carry = list(carry)
        for u in units:
            n, hh = u
            r0 = specs[n][1]
            rest, acc = carry[2 * hh], carry[2 * hh + 1]
            a = jnp.exp(log_beta[u] + after[u] + jnp.concatenate([rest[r0:]] * (tk // LANES), axis=1))
            if masked:
                a = jnp.where(causal[n], a, 0.0)
            acc_new = acc[r0:] + _dot(a.astype(BF16), vbs[n])
            rest_new = rest[r0:] + jnp.broadcast_to(after[u][:, 0:1] + log_rest[u][:, 0:1], (tq - r0, LANES))
            carry[2 * hh + 1] = acc_new if r0 == 0 else jnp.concatenate([acc[:r0], acc_new], axis=0)
            carry[2 * hh] = rest_new if r0 == 0 else jnp.concatenate([rest[:r0], rest_new], axis=0)
        return tuple(carry)

    carry = (jnp.zeros((tq, LANES), F32), jnp.zeros((tq, LANES), F32)) * 2
    carry = blocks([(i * ratio + d, d * tk) for d in range(ratio - 1, -1, -1)], carry, True)
    carry = lax.fori_loop(
        0, i, lambda t, c: blocks([((i - t) * ratio - 1 - d, 0) for d in range(ratio)], c, False), carry)
    o_ref[...] = jnp.where(lane < SB_HEAD_DIM, carry[1], carry[3]).astype(BF16)


def _stick_breaking(sq, sk, sv, B, S):
    nq = S // SB_TQ
    pairs = SB_WIDTH // LANES
    q_spec = pl.BlockSpec((SB_TQ, LANES), lambda b, p, i: (b * nq + i, p))
    kv_spec = pl.BlockSpec((S, LANES), lambda b, p, i: (b, p))
    return pl.pallas_call(
        _stick_kernel,
        out_shape=jax.ShapeDtypeStruct((B * S, SB_WIDTH), BF16),
        grid=(B, pairs, nq),
        in_specs=[q_spec, kv_spec, kv_spec],
        out_specs=q_spec,
        compiler_params=_params(("parallel", "parallel", "arbitrary")),
        name="stick_breaking_attention",
    )(sq, sk, sv)


def _layer_norm(v, g, b):
    mu = jnp.mean(v, -1, keepdims=True)
    c = v - mu
    var = jnp.mean(c * c, -1, keepdims=True)
    return c * lax.rsqrt(var + LN_EPS) * g + b


def _silu(v):
    return v * jax.nn.sigmoid(v)


def _merge_kernel(oa_ref, ob_ref, ga_ref, gb_ref, x_ref, wpa_ref, wpb_ref, wout_ref, wr_ref, wsu_ref, wsd_ref,
                  g1_ref, b1_ref, rbias_ref, h_ref, hs_ref, idx_ref, gate_ref, rank_ref, cnt_ref, run_ref):
    ya = _dot(oa_ref[...], wpa_ref[...])
    yb = _dot(ob_ref[...], wpb_ref[...])
    merged = jax.nn.sigmoid(ga_ref[...].astype(F32)) * ya + jax.nn.sigmoid(gb_ref[...].astype(F32)) * yb
    pre = DEEPNORM_ALPHA * x_ref[...] + _dot(merged.astype(BF16), wout_ref[...])
    h = _layer_norm(pre, g1_ref[...], b1_ref[...])
    _store_row_tiles(h_ref, h)
    hb = h.astype(BF16)
    scores = jax.nn.sigmoid(_dot(hb, wr_ref[...]))
    up = _dot(hb, wsu_ref[...])
    act = _silu(up[:, :SHARED_FF]) * up[:, SHARED_FF:]
    hs_ref[...] = DEEPNORM_ALPHA * h + _dot(act.astype(BF16), wsd_ref[...])
    _route_tile(scores, rbias_ref, idx_ref, gate_ref, rank_ref, cnt_ref, run_ref)


def _merge(oa, ob, ga, gb, xf, wpa, wpb, wout, wr, wsu, wsd, g1, b1, rbias, tm=256):
    T = xf.shape[0]
    row = lambda w: pl.BlockSpec((tm, w), lambda i: (i, 0))
    return pl.pallas_call(
        _merge_kernel,
        out_shape=[jax.ShapeDtypeStruct((T * ROW_TILE, LANES), U32), jax.ShapeDtypeStruct((T, D_MODEL), F32),
                   jax.ShapeDtypeStruct((T, LANES), I32), jax.ShapeDtypeStruct((T, LANES), F32),
                   jax.ShapeDtypeStruct((T, LANES), I32), jax.ShapeDtypeStruct((1, N_EXPERTS), F32)],
        grid=(T // tm,),
        in_specs=[row(DN_WIDTH), row(SB_WIDTH), row(D_MODEL), row(D_MODEL), row(D_MODEL),
                  _const_spec(wpa.shape), _const_spec(wpb.shape), _const_spec(wout.shape), _const_spec(wr.shape),
                  _const_spec(wsu.shape), _const_spec(wsd.shape), _const_spec(g1.shape), _const_spec(b1.shape),
                  _const_spec(rbias.shape)],
        out_specs=[pl.BlockSpec((tm * ROW_TILE, LANES), lambda i: (i, 0)), row(D_MODEL), row(LANES), row(LANES),
                   row(LANES), pl.BlockSpec((1, N_EXPERTS), lambda i: (0, 0))],
        scratch_shapes=[pltpu.VMEM((1, N_EXPERTS), F32)],
        compiler_params=_params(("arbitrary",)),
        name="merge_norm_shared_route",
    )(oa, ob, ga, gb, xf, wpa, wpb, wout, wr, wsu, wsd, g1, b1, rbias)


def _route_tile(scores, bias_ref, idx_ref, gate_ref, rank_ref, cnt_ref, run_ref):
    i = pl.program_id(0)
    tm = scores.shape[0]

    @pl.when(i == 0)
    def _():
        run_ref[...] = jnp.zeros_like(run_ref)

    sel = scores + bias_ref[...]
    lane_i = lax.broadcasted_iota(I32, (tm, N_EXPERTS), 1)
    grp = lane_i // GROUP_SIZE
    lane = lane_i.astype(F32)
    neg = F32(-jnp.inf)

    def first_argmax(v):
        m = jnp.max(v, -1, keepdims=True)
        ix = jnp.min(jnp.where(v == m, lane, float(N_EXPERTS)), -1, keepdims=True)
        return m, ix

    gscore = []
    for g in range(N_GROUPS):
        gv = jnp.where(grp == g, sel, neg)
        m1, i1 = first_argmax(gv)
        m2 = jnp.max(jnp.where(lane == i1, neg, gv), -1, keepdims=True)
        gscore.append(m1 + m2)
    emask = jnp.zeros((tm, N_EXPERTS), jnp.bool_)
    for g in range(N_GROUPS):
        beaten = jnp.zeros((tm, 1), I32)
        for o in range(N_GROUPS):
            if o == g:
                continue
            wins = (gscore[o] > gscore[g]) | ((gscore[o] == gscore[g]) & (o < g))
            beaten = beaten + wins.astype(I32)
        emask = emask | ((grp == g) & (beaten < TOP_GROUPS))
    masked = jnp.where(emask, sel, neg)

    picked = jnp.zeros((tm, N_EXPERTS), jnp.bool_)
    idx_cols, score_cols, hots = [], [], []
    for _ in range(TOP_K):
        _, ix = first_argmax(masked)
        hot = lane == ix
        idx_cols.append(ix)
        score_cols.append(jnp.sum(jnp.where(hot, scores, 0.0), -1, keepdims=True))
        hots.append(hot)
        masked = jnp.where(hot, neg, masked)
        picked = picked | hot
    chosen = jnp.where(picked, 1.0, 0.0).astype(BF16)
    r_io = lax.broadcasted_iota(I32, (tm, tm), 0)
    c_io = lax.broadcasted_iota(I32, (tm, tm), 1)
    strict_lower = jnp.where(c_io < r_io, 1.0, 0.0).astype(BF16)
    before = _dot(strict_lower, chosen) + run_ref[...]
    run_ref[...] = run_ref[...] + jnp.sum(chosen.astype(F32), 0, keepdims=True)
    cnt_ref[...] = run_ref[...]

    total = score_cols[0]
    for s in score_cols[1:]:
        total = total + s
    lane_o = lax.broadcasted_iota(I32, (tm, LANES), 1)
    idx_o = jnp.zeros((tm, LANES), F32)
    gate_o = jnp.zeros((tm, LANES), F32)
    rank_o = jnp.zeros((tm, LANES), F32)
    for k in range(TOP_K):
        rk = jnp.sum(jnp.where(hots[k], before, 0.0), -1, keepdims=True)
        idx_o = jnp.where(lane_o == k, idx_cols[k], idx_o)
        gate_o = jnp.where(lane_o == k, score_cols[k] / total * ROUTED_SCALE, gate_o)
        rank_o = jnp.where(lane_o == k, rk, rank_o)
    idx_ref[...] = idx_o.astype(I32)
    gate_ref[...] = gate_o
    rank_ref[...] = rank_o.astype(I32)


def _dest_kernel(idx_ref, rank_ref, pstart_ref, dest_ref):
    tm = idx_ref.shape[0]
    idx = idx_ref[...]
    lane = lax.broadcasted_iota(I32, (tm, N_EXPERTS), 1)
    lane_o = lax.broadcasted_iota(I32, (tm, LANES), 1)
    start = jnp.zeros((tm, LANES), F32)
    for k in range(TOP_K):
        sk = jnp.sum(jnp.where(lane == idx[:, k:k + 1], pstart_ref[...], 0.0), -1, keepdims=True)
        start = jnp.where(lane_o == k, sk, start)
    dest_ref[...] = start.astype(I32) + rank_ref[...]


def _dest_rows(idx, rank, pstart_f, tm=512):
    T = idx.shape[0]
    row = pl.BlockSpec((tm, LANES), lambda i: (i, 0))
    return pl.pallas_call(
        _dest_kernel,
        out_shape=jax.ShapeDtypeStruct((T, LANES), I32),
        grid=(T // tm,),
        in_specs=[row, row, pl.BlockSpec((1, N_EXPERTS), lambda i: (0, 0))],
        out_specs=row,
        compiler_params=_params(("arbitrary",)),
        name="dest_rows",
    )(idx, rank, pstart_f)


DISPATCH_TOKENS = 512
COMBINE_TOKENS = 128
EXPERT_ROWS = 256


def _dispatch_kernel(dest_ref, h_ref, xs_hbm, sem):
    def issue(t, carry):
        src = pl.multiple_of(t * ROW_TILE, ROW_TILE)
        for k in range(TOP_K):
            dst = pl.multiple_of(dest_ref[t * TOP_K + k] * ROW_TILE, ROW_TILE)
            pltpu.make_async_copy(h_ref.at[pl.ds(src, ROW_TILE)], xs_hbm.at[pl.ds(dst, ROW_TILE)],
                                  sem).start(priority=k % 2)
        return carry

    lax.fori_loop(0, DISPATCH_TOKENS, issue, 0)
    for _ in range(TOP_K):
        pltpu.make_async_copy(h_ref, xs_hbm.at[pl.ds(0, DISPATCH_TOKENS * ROW_TILE)], sem).wait()


def _dispatch(dest_flat, h_rt, n_rows):
    T = h_rt.shape[0] // ROW_TILE
    tm = DISPATCH_TOKENS
    return pl.pallas_call(
        _dispatch_kernel,
        out_shape=jax.ShapeDtypeStruct((n_rows * ROW_TILE, LANES), U32),
        grid=(T // tm,),
        in_specs=[pl.BlockSpec((tm * TOP_K,), lambda i: (i,), memory_space=pltpu.SMEM),
                  pl.BlockSpec((tm * ROW_TILE, LANES), lambda i: (i, 0))],
        out_specs=pl.BlockSpec(memory_space=pl.ANY),
        scratch_shapes=[pltpu.SemaphoreType.DMA(())],
        compiler_params=pltpu.CompilerParams(dimension_semantics=("arbitrary",), has_side_effects=True,
                                             vmem_limit_bytes=VMEM_LIMIT),
        name="dispatch_rows",
    )(dest_flat, h_rt)


def _expert_kernel(blk_ref, be_ref, first_ref, slot_ref, nexte_ref, nu_ref, xs_ref, wu_hbm, wd_hbm, ys_ref,
                   wu32_ref, wd32_ref, wu16_ref, wd16_ref, sems):
    b = pl.program_id(0)
    used = b < nu_ref[0]

    def weight_copies(e, s):
        return (pltpu.make_async_copy(wu_hbm.at[e], wu32_ref.at[s], sems.at[0, s]),
                pltpu.make_async_copy(wd_hbm.at[e], wd32_ref.at[s], sems.at[1, s]))

    @pl.when(b == 0)
    def _():
        for c in weight_copies(be_ref[0], 0):
            c.start()

    @pl.when(used & (first_ref[b] == 1))
    def _():
        s = slot_ref[b]
        for c in weight_copies(be_ref[b], s):
            c.wait()
        wu16_ref[...] = wu32_ref[s].astype(BF16)
        wd16_ref[...] = wd32_ref[s].astype(BF16)

        @pl.when(nexte_ref[b] >= 0)
        def _():
            for c in weight_copies(nexte_ref[b], 1 - s):
                c.start()

    @pl.when(used)
    def _():
        xb = _load_row_tiles(xs_ref, EXPERT_ROWS).astype(BF16)
        up = _dot(xb, wu16_ref[...])
        act = _silu(up[:, :EXPERT_FF]) * up[:, EXPERT_FF:]
        _store_row_tiles(ys_ref, _dot(act.astype(BF16), wd16_ref[...]))


def _experts(block_idx, block_e, first, slot, next_e, n_used, xs, w_up, w_down):
    n_blocks = xs.shape[0] // (EXPERT_ROWS * ROW_TILE)
    rows = pl.BlockSpec((EXPERT_ROWS * ROW_TILE, LANES), lambda b, bi, *_: (bi[b], 0))
    grid_spec = pltpu.PrefetchScalarGridSpec(
        num_scalar_prefetch=6,
        grid=(n_blocks,),
        in_specs=[rows, pl.BlockSpec(memory_space=pl.ANY), pl.BlockSpec(memory_space=pl.ANY)],
        out_specs=rows,
        scratch_shapes=[pltpu.VMEM((2, D_MODEL, 2 * EXPERT_FF), F32), pltpu.VMEM((2, EXPERT_FF, D_MODEL), F32),
                        pltpu.VMEM((D_MODEL, 2 * EXPERT_FF), BF16), pltpu.VMEM((EXPERT_FF, D_MODEL), BF16),
                        pltpu.SemaphoreType.DMA((2, 2))],
    )
    return pl.pallas_call(
        _expert_kernel,
        out_shape=jax.ShapeDtypeStruct(xs.shape, U32),
        grid_spec=grid_spec,
        compiler_params=_params(("arbitrary",)),
        name="routed_experts",
    )(block_idx, block_e, first, slot, next_e, n_used, xs, w_up, w_down)


def _combine_kernel(dest_ref, destn_ref, gate_ref, hs_ref, g2_ref, b2_ref, ys_hbm, out_ref, ybuf, sems):
    i = pl.program_id(0)
    n = pl.num_programs(0)
    slot = i % 2

    def gather(dr, s):
        def issue(t, carry):
            dst = pl.multiple_of(t * ROW_TILE, ROW_TILE)
            for k in range(TOP_K):
                src = pl.multiple_of(dr[t * TOP_K + k] * ROW_TILE, ROW_TILE)
                pltpu.make_async_copy(ys_hbm.at[pl.ds(src, ROW_TILE)], ybuf.at[s, k, pl.ds(dst, ROW_TILE)],
                                      sems.at[s]).start(priority=k % 2)
            return carry

        lax.fori_loop(0, COMBINE_TOKENS, issue, 0)

    @pl.when(i == 0)
    def _():
        gather(dest_ref, 0)

    @pl.when(i + 1 < n)
    def _():
        gather(destn_ref, 1 - slot)

    for k in range(TOP_K):
        pltpu.make_async_copy(ys_hbm.at[pl.ds(0, COMBINE_TOKENS * ROW_TILE)], ybuf.at[slot, k],
                              sems.at[slot]).wait()

    gates = gate_ref[...]
    acc = hs_ref[...]
    for k in range(TOP_K):
        acc = acc + gates[:, k:k + 1] * _load_row_tiles(ybuf, COMBINE_TOKENS, (slot, k))
    out_ref[...] = _layer_norm(acc, g2_ref[...], b2_ref[...])


def _combine(dest_flat, gates, hs, g2, b2, ys):
    T = hs.shape[0]
    tm = COMBINE_TOKENS
    n = T // tm
    cur = pl.BlockSpec((tm * TOP_K,), lambda i: (i,), memory_space=pltpu.SMEM)
    nxt = pl.BlockSpec((tm * TOP_K,), lambda i: (jnp.minimum(i + 1, n - 1),), memory_space=pltpu.SMEM)
    return pl.pallas_call(
        _combine_kernel,
        out_shape=jax.ShapeDtypeStruct((T, D_MODEL), F32),
        grid=(n,),
        in_specs=[cur, nxt, pl.BlockSpec((tm, LANES), lambda i: (i, 0)), pl.BlockSpec((tm, D_MODEL), lambda i: (i, 0)),
                  pl.BlockSpec((1, D_MODEL), lambda i: (0, 0)), pl.BlockSpec((1, D_MODEL), lambda i: (0, 0)),
                  pl.BlockSpec(memory_space=pl.ANY)],
        out_specs=pl.BlockSpec((tm, D_MODEL), lambda i: (i, 0)),
        scratch_shapes=[pltpu.VMEM((2, TOP_K, tm * ROW_TILE, LANES), U32), pltpu.SemaphoreType.DMA((2,))],
        compiler_params=_params(("arbitrary",)),
        name="combine_norm",
    )(dest_flat, dest_flat, gates, hs, g2, b2, ys)


def _pack_w_in(w_in):
    c1 = 4 * DN_WIDTH
    c2 = c1 + 2 * DN_HEADS
    ba = jnp.pad(w_in[:, c1:c2], ((0, 0), (0, LANES - 2 * DN_HEADS)))
    return jnp.concatenate([w_in[:, :c1], ba, w_in[:, c2:]], axis=1).astype(BF16)


def _head_lanes(v):
    return jnp.pad(v.astype(F32), (DN_HEADS, LANES - 2 * DN_HEADS)).reshape(1, LANES)


def _layer(x, w_in, conv_w, dn_a_log, dn_dt_bias, dn_norm_g, w_proj_a, w_proj_b, w_out, ln1_g, ln1_b, w_router,
           router_bias, w_shared_up, w_shared_down, w_expert_up, w_expert_down, ln2_g, ln2_b):
    B, S, D = x.shape
    T = B * S
    xf = x.reshape(T, D)
    qkv, z, ba, sq, sk, sv, ga, gb = _input_projection(xf, _pack_w_in(w_in))

    oa = _deltanet(qkv, z, ba, conv_w, _head_lanes(dn_a_log), _head_lanes(dn_dt_bias),
                   dn_norm_g.reshape(1, DN_HEAD_DIM), B, S)
    ob = _stick_breaking(sq, sk, sv, B, S)

    h, hs, idx, gates, rank, counts = _merge(
        oa, ob, ga, gb, xf, w_proj_a.astype(BF16), w_proj_b.astype(BF16), w_out.astype(BF16), w_router.astype(BF16),
        w_shared_up.astype(BF16), w_shared_down.astype(BF16), ln1_g.reshape(1, D), ln1_b.reshape(1, D),
        router_bias.reshape(1, N_EXPERTS).astype(F32))

    out = _routed_path(h, hs, idx, gates, rank, counts, w_expert_up, w_expert_down, ln2_g.reshape(1, D),
                       ln2_b.reshape(1, D))
    return out.reshape(B, S, D)


def _routed_path(h, hs, idx, gates, rank, counts, w_expert_up, w_expert_down, g2, b2):
    T = hs.shape[0]
    counts = counts.reshape(N_EXPERTS).astype(I32)
    padded = (counts + EXPERT_ROWS - 1) // EXPERT_ROWS * EXPERT_ROWS
    pend = jnp.cumsum(padded)
    pstart = (pend - padded).astype(I32)
    n_blocks = -(-(T * TOP_K + N_EXPERTS * (EXPERT_ROWS - 1)) // EXPERT_ROWS)
    n_used = (pend[-1] // EXPERT_ROWS).astype(I32)
    block_idx = jnp.minimum(jnp.arange(n_blocks, dtype=I32), n_used - 1)
    block_e = jnp.minimum(jnp.sum(pend[None, :] <= (block_idx * EXPERT_ROWS)[:, None], 1), N_EXPERTS - 1).astype(I32)

    prev_e = jnp.concatenate([jnp.full((1,), -1, I32), block_e[:-1]])
    first = (block_e != prev_e).astype(I32)
    slot = ((jnp.cumsum(first) - 1) % 2).astype(I32)
    expert_ids = jnp.arange(N_EXPERTS, dtype=I32)
    later_used = jnp.where(counts > 0, expert_ids, N_EXPERTS)
    next_used = lax.cummin(jnp.concatenate([later_used[1:], jnp.full((1,), N_EXPERTS, I32)]), reverse=True)
    next_used = jnp.where(next_used < N_EXPERTS, next_used, -1)
    next_e = next_used[block_e].astype(I32)

    dest = _dest_rows(idx, rank, pstart.astype(F32).reshape(1, N_EXPERTS))
    dest_flat = dest[:, :TOP_K].reshape(T * TOP_K)
    xs = _dispatch(dest_flat, h, n_blocks * EXPERT_ROWS)
    ys = _experts(block_idx, block_e, first, slot, next_e, n_used.reshape(1), xs, w_expert_up, w_expert_down)
    return _combine(dest_flat, gates, hs, g2, b2, ys)


def kernel(x, w_in, conv_w, dn_a_log, dn_dt_bias, dn_norm_g, w_proj_a, w_proj_b, w_out, ln1_g, ln1_b, w_router,
           router_bias, w_shared_up, w_shared_down, w_expert_up, w_expert_down, ln2_g, ln2_b):
    for layer in range(DEPTH):
        x = _layer(x, w_in[layer], conv_w[layer], dn_a_log[layer], dn_dt_bias[layer], dn_norm_g[layer],
                   w_proj_a[layer], w_proj_b[layer], w_out[layer], ln1_g[layer], ln1_b[layer], w_router[layer],
                   router_bias[layer], w_shared_up[layer], w_shared_down[layer], w_expert_up[layer],
                   w_expert_down[layer], ln2_g[layer], ln2_b[layer])
    return x
```

```python
import functools
import math

import jax
import jax.numpy as jnp
import numpy as np
from jax import lax
from jax.experimental import pallas as pl
from jax.experimental.pallas import tpu as pltpu

F32 = jnp.float32
BF16 = jnp.bfloat16
I32 = jnp.int32

D_MODEL = 1024
CHUNK = 64
DN_HEADS = 4
DN_HEAD_DIM = 128
DN_WIDTH = DN_HEADS * DN_HEAD_DIM
CONV_WIDTH = 4
SB_HEADS = 8
SB_HEAD_DIM = 64
SB_WIDTH = SB_HEADS * SB_HEAD_DIM
N_EXPERTS = 256
TOP_K = 8
N_GROUPS = 8
GROUP_SIZE = N_EXPERTS // N_GROUPS
TOP_GROUPS = 4
EXPERT_FF = 256
SHARED_FF = 256
ROUTED_SCALE = 2.5
EXPERT_BLOCK = 128
DEPTH = 1
DEEPNORM_ALPHA = (2 * DEPTH) ** 0.25
LN_EPS = 1e-5
NORM_EPS = 1e-6

LANES = 128
SUBLANES = 8
VMEM_LIMIT = 56 * 1024 * 1024

_C_QKV = (0, 3 * DN_WIDTH)
_C_Z = (_C_QKV[1], _C_QKV[1] + DN_WIDTH)
_C_BA = (_C_Z[1], _C_Z[1] + LANES)
_C_SQ = (_C_BA[1], _C_BA[1] + SB_WIDTH)
_C_SK = (_C_SQ[1], _C_SQ[1] + SB_WIDTH)
_C_SV = (_C_SK[1], _C_SK[1] + SB_WIDTH)
_C_GA = (_C_SV[1], _C_SV[1] + D_MODEL)
_C_GB = (_C_GA[1], _C_GA[1] + D_MODEL)
PROJ_PACKED = _C_GB[1]


def _params(sem, vmem=VMEM_LIMIT):
    return pltpu.CompilerParams(dimension_semantics=sem, vmem_limit_bytes=vmem)


def _const_spec(shape):
    return pl.BlockSpec(shape, lambda *_: (0,) * len(shape), pipeline_mode=pl.Buffered(1))


def _dot(a, b):
    return jnp.dot(a, b, preferred_element_type=F32)


def _dot_nt(a, b):
    return lax.dot_general(a, b, (((1,), (1,)), ((), ())), preferred_element_type=F32)


def _split_bf16(a):
    hi = a.astype(BF16)
    return hi, (a - hi.astype(F32)).astype(BF16)


U32 = jnp.uint32
HALF_MODEL = D_MODEL // 2
ROW_TILE = HALF_MODEL // LANES
HIGH_HALF_MASK = 0xFFFF0000


def _store_row_tiles(ref, mat):
    n = mat.shape[0]
    lo = lax.bitcast_convert_type(mat[:, :HALF_MODEL].astype(BF16).astype(F32), U32)
    hi = lax.bitcast_convert_type(mat[:, HALF_MODEL:].astype(BF16).astype(F32), U32)
    words = (lo >> 16) | hi
    for s in range(ROW_TILE):
        ref[pl.ds(s, n, stride=ROW_TILE), :] = words[:, s * LANES:(s + 1) * LANES]


def _load_row_tiles(ref, n, lead=()):
    words = [ref[lead + (pl.ds(s, n, stride=ROW_TILE), slice(None))] for s in range(ROW_TILE)]
    lo = [lax.bitcast_convert_type(w << 16, F32) for w in words]
    hi = [lax.bitcast_convert_type(w & U32(HIGH_HALF_MASK), F32) for w in words]
    return jnp.concatenate(lo + hi, axis=1)


def _proj_kernel(x_ref, w_ref, qkv_ref, z_ref, ba_ref, sq_ref, sk_ref, sv_ref, ga_ref, gb_ref):
    xb = x_ref[...].astype(BF16)

    def mm(c):
        return _dot(xb, w_ref[:, c[0]:c[1]])

    qkv_ref[...] = mm(_C_QKV)
    z_ref[...] = mm(_C_Z)
    ba_ref[...] = mm(_C_BA)
    sq_ref[...] = mm(_C_SQ).astype(BF16)
    sk_ref[...] = mm(_C_SK).astype(BF16)
    sv_ref[...] = mm(_C_SV).astype(BF16)
    ga_ref[...] = mm(_C_GA)
    gb_ref[...] = mm(_C_GB)


def _input_projection(xf, w_packed, tm=512):
    T = xf.shape[0]
    widths = [(3 * DN_WIDTH, F32), (DN_WIDTH, F32), (LANES, F32), (SB_WIDTH, BF16), (SB_WIDTH, BF16),
              (SB_WIDTH, BF16), (D_MODEL, F32), (D_MODEL, F32)]
    return pl.pallas_call(
        _proj_kernel,
        out_shape=[jax.ShapeDtypeStruct((T, w), dt) for w, dt in widths],
        grid=(T // tm,),
        in_specs=[pl.BlockSpec((tm, D_MODEL), lambda i: (i, 0)), _const_spec((D_MODEL, PROJ_PACKED))],
        out_specs=[pl.BlockSpec((tm, w), lambda i: (i, 0)) for w, _ in widths],
        compiler_params=_params(("parallel",)),
        name="input_projection",
    )(xf, w_packed)


DN_STEP_CHUNKS = 4


def _deltanet_kernel(qkv_ref, z_ref, ba_ref, cw_ref, alog_ref, dtb_ref, ng_ref, o_ref, win_ref, st_ref):
    n = pl.program_id(1)
    C = CHUNK
    R = DN_STEP_CHUNKS * C
    hd = DN_HEAD_DIM

    @pl.when(n == 0)
    def _():
        win_ref[0:SUBLANES, :] = jnp.zeros((SUBLANES, 3 * DN_WIDTH), F32)
        st_ref[...] = jnp.zeros_like(st_ref)

    win_ref[SUBLANES:SUBLANES + R, :] = qkv_ref[...]
    conv = win_ref[pl.ds(SUBLANES - (CONV_WIDTH - 1), R), :] * cw_ref[0:1, :]
    for tap in range(1, CONV_WIDTH):
        conv = conv + win_ref[pl.ds(SUBLANES - (CONV_WIDTH - 1) + tap, R), :] * cw_ref[tap:tap + 1, :]
    win_ref[0:SUBLANES, :] = win_ref[R:R + SUBLANES, :]
    act_all = _silu(conv)

    bt = ba_ref[...]
    beta_rows = jax.nn.sigmoid(bt)
    g_rows = -jnp.exp(alog_ref[...]) * jax.nn.softplus(bt + dtb_ref[...])
    r_io = lax.broadcasted_iota(I32, (C, C), 0)
    c_io = lax.broadcasted_iota(I32, (C, C), 1)
    incl = r_io >= c_io
    strict = r_io > c_io
    tri = jnp.where(incl, 1.0, 0.0).astype(BF16)
    eye = jnp.where(r_io == c_io, 1.0, 0.0).astype(F32)
    pairs = [(c, h) for c in range(DN_STEP_CHUNKS) for h in range(DN_HEADS)]
    G, eG, eGd, eGl, Gt = [], [], [], [], []
    for c in range(DN_STEP_CHUNKS):
        g_hi, g_lo = _split_bf16(g_rows[c * C:(c + 1) * C])
        G.append(_dot(tri, g_hi) + _dot(tri, g_lo))
    for c in range(DN_STEP_CHUNKS):
        G_last = G[c][C - 1:C, :]
        eG.append(jnp.exp(G[c]))
        eGd.append(jnp.exp(G_last - G[c]))
        eGl.append(jnp.exp(G_last))
        Gt.append(jnp.concatenate([G[c], jnp.zeros((LANES - C, LANES), F32)], axis=0).T)

    q, k, k16, kb, v, beta, decay = {}, {}, {}, {}, {}, {}, {}
    for c, h in pairs:
        act = act_all[c * C:(c + 1) * C]
        qq = act[:, h * hd:(h + 1) * hd]
        kk = act[:, DN_WIDTH + h * hd:DN_WIDTH + (h + 1) * hd]
        v[c, h] = act[:, 2 * DN_WIDTH + h * hd:2 * DN_WIDTH + (h + 1) * hd]
        q[c, h] = qq * lax.rsqrt(jnp.sum(qq * qq, -1, keepdims=True) + NORM_EPS) * (hd ** -0.5)
        k[c, h] = kk * lax.rsqrt(jnp.sum(kk * kk, -1, keepdims=True) + NORM_EPS)
        beta[c, h] = beta_rows[c * C:(c + 1) * C, h:h + 1]
        gl = DN_HEADS + h
        diff = G[c][:, gl:gl + 1] - Gt[c][gl:gl + 1, 0:C]
        decay[c, h] = jnp.where(incl, jnp.exp(jnp.where(incl, diff, 0.0)), 0.0)
        kb[c, h] = k[c, h] * beta[c, h]
        k16[c, h] = k[c, h].astype(BF16)
    M = {p: jnp.where(strict, _dot_nt(kb[p].astype(BF16), k16[p]) * decay[p], 0.0) for p in pairs}
    attn = {p: (_dot_nt(q[p].astype(BF16), k16[p]) * decay[p]).astype(BF16) for p in pairs}
    P = {p: eye - M[p] for p in pairs}
    Mp16 = {p: M[p].astype(BF16) for p in pairs}
    for _ in range(5):
        Mp16 = {p: _dot(Mp16[p], Mp16[p]).astype(BF16) for p in pairs}
        P = {p: P[p] + _dot(P[p].astype(BF16), Mp16[p]) for p in pairs}
    sol = {}
    for c, h in pairs:
        gl = DN_HEADS + h
        rhs = jnp.concatenate([v[c, h] * beta[c, h], kb[c, h] * eG[c][:, gl:gl + 1]], axis=1)
        sol[c, h] = _dot(P[c, h].astype(BF16), rhs.astype(BF16))

    heads = range(DN_HEADS)
    S = [st_ref[h] for h in heads]
    for c in range(DN_STEP_CHUNKS):
        wq, vn, o = [], [], []
        for h in heads:
            gl = DN_HEADS + h
            qd = q[c, h] * eG[c][:, gl:gl + 1]
            wq.append(_dot(jnp.concatenate([sol[c, h][:, hd:], qd], axis=0).astype(BF16), S[h].astype(BF16)))
        for h in heads:
            vn.append((sol[c, h][:, :hd] - wq[h][:C]).astype(BF16))
        for h in heads:
            o.append(wq[h][C:] + _dot(attn[c, h], vn[h]))
        for h in heads:
            gl = DN_HEADS + h
            kd = k[c, h] * eGd[c][:, gl:gl + 1]
            S[h] = S[h] * eGl[c][:, gl:gl + 1] + _dot(kd.T.astype(BF16), vn[h])
        outs = []
        for h in heads:
            on = o[h] * lax.rsqrt(jnp.mean(o[h] * o[h], -1, keepdims=True) + NORM_EPS) * ng_ref[...]
            outs.append(on * _silu(z_ref[c * C:(c + 1) * C, h * hd:(h + 1) * hd].astype(F32)))
        o_ref[c * C:(c + 1) * C, :] = jnp.concatenate(outs, axis=1).astype(BF16)
    for h in heads:
        st_ref[h] = S[h]


def _deltanet(qkv, z, ba, conv_w, alog_l, dtb_l, norm_g, B, S):
    rows = DN_STEP_CHUNKS * CHUNK
    N = S // rows
    row = lambda w: pl.BlockSpec((rows, w), lambda b, n: (b * N + n, 0))
    const = lambda s: pl.BlockSpec(s, lambda b, n: (0, 0))
    return pl.pallas_call(
        _deltanet_kernel,
        out_shape=jax.ShapeDtypeStruct((B * S, DN_WIDTH), BF16),
        grid=(B, N),
        in_specs=[row(3 * DN_WIDTH), row(DN_WIDTH), row(LANES), const((CONV_WIDTH, 3 * DN_WIDTH)),
                  const((1, LANES)), const((1, LANES)), const((1, DN_HEAD_DIM))],
        out_specs=row(DN_WIDTH),
        scratch_shapes=[pltpu.VMEM((rows + SUBLANES, 3 * DN_WIDTH), F32),
                        pltpu.VMEM((DN_HEADS, DN_HEAD_DIM, DN_HEAD_DIM), F32)],
        compiler_params=_params(("parallel", "arbitrary")),
        name="gated_deltanet",
    )(qkv, z, ba, conv_w, alog_l, dtb_l, norm_g)


SB_TQ = 512
SB_TK = 256


def _stick_kernel(q_ref, k_ref, v_ref, o_ref):
    i = pl.program_id(2)
    tq, tk = SB_TQ, SB_TK
    ratio = tq // tk
    lane = lax.broadcasted_iota(I32, (tq, LANES), 1)
    u_r = lax.broadcasted_iota(I32, (tk, tk), 0)
    u_c = lax.broadcasted_iota(I32, (tk, tk), 1)
    later = jnp.where(u_r > u_c, 1.0, 0.0).astype(BF16)
    q = q_ref[...] * (SB_HEAD_DIM ** -0.5)
    zero = jnp.zeros_like(q)
    qs = [jnp.where((lane // SB_HEAD_DIM) == hh, q, zero) for hh in range(2)]

    def blocks(specs, carry, masked):
        kbs, vbs, causal = [], [], []
        for j, r0 in specs:
            off = pl.multiple_of(j * tk, tk)
            kbs.append(k_ref[pl.ds(off, tk), :])
            vbs.append(v_ref[pl.ds(off, tk), :])
            if masked:
                q_pos = lax.broadcasted_iota(I32, (tq - r0, tk), 0) + (r0 + i * tq)
                k_pos = lax.broadcasted_iota(I32, (tq - r0, tk), 1) + j * tk
                causal.append(k_pos < q_pos)
            else:
                causal.append(None)
        units = [(n, hh) for n in range(len(specs)) for hh in range(2)]
        zs = {u: _dot_nt(qs[u[1]][specs[u[0]][1]:], kbs[u[0]]) for u in units}
        log_beta, log_rest = {}, {}
        for u in units:
            lb = jnp.minimum(zs[u], 0.0) - jnp.log(1.0 + jnp.exp(-jnp.abs(zs[u])))
            lr = lb - zs[u]
            if masked:
                lr = jnp.where(causal[u[0]], lr, 0.0)
            log_beta[u], log_rest[u] = lb, lr
        after = {}
        for u in units:
            after[u] = _dot(log_rest[u].astype(BF16), later)
        carry = list(carry)
        for u in units:
            n, hh = u
            r0 = specs[n][1]
            rest, acc = carry[2 * hh], carry[2 * hh + 1]
            a = jnp.exp(log_beta[u] + after[u] + jnp.concatenate([rest[r0:]] * (tk // LANES), axis=1))
            if masked:
                a = jnp.where(causal[n], a, 0.0)
            acc_new = acc[r0:] + _dot(a.astype(BF16), vbs[n])
            rest_new = rest[r0:] + jnp.broadcast_to(after[u][:, 0:1] + log_rest[u][:, 0:1], (tq - r0, LANES))
            carry[2 * hh + 1] = acc_new if r0 == 0 else jnp.concatenate([acc[:r0], acc_new], axis=0)
            carry[2 * hh] = rest_new if r0 == 0 else jnp.concatenate([rest[:r0], rest_new], axis=0)
        return tuple(carry)

    carry = (jnp.zeros((tq, LANES), F32), jnp.zeros((tq, LANES), F32)) * 2
    carry = blocks([(i * ratio + d, d * tk) for d in range(ratio - 1, -1, -1)], carry, True)
    carry = lax.fori_loop(
        0, i, lambda t, c: blocks([((i - t) * ratio - 1 - d, 0) for d in range(ratio)], c, False), carry)
    o_ref[...] = jnp.where(lane < SB_HEAD_DIM, carry[1], carry[3]).astype(BF16)


def _stick_breaking(sq, sk, sv, B, S):
    nq = S // SB_TQ
    pairs = SB_WIDTH // LANES
    q_spec = pl.BlockSpec((SB_TQ, LANES), lambda b, p, i: (b * nq + i, p))
    kv_spec = pl.BlockSpec((S, LANES), lambda b, p, i: (b, p))
    return pl.pallas_call(
        _stick_kernel,
        out_shape=jax.ShapeDtypeStruct((B * S, SB_WIDTH), BF16),
        grid=(B, pairs, nq),
        in_specs=[q_spec, kv_spec, kv_spec],
        out_specs=q_spec,
        compiler_params=_params(("parallel", "parallel", "arbitrary")),
        name="stick_breaking_attention",
    )(sq, sk, sv)


def _layer_norm(v, g, b):
    mu = jnp.mean(v, -1, keepdims=True)
    c = v - mu
    var = jnp.mean(c * c, -1, keepdims=True)
    return c * lax.rsqrt(var + LN_EPS) * g + b


def _silu(v):
    return v * jax.nn.sigmoid(v)


def _merge_kernel(oa_ref, ob_ref, ga_ref, gb_ref, x_ref, wpa_ref, wpb_ref, wout_ref, wr_ref, wsu_ref, wsd_ref,
                  g1_ref, b1_ref, rbias_ref, h_ref, hs_ref, idx_ref, gate_ref, rank_ref, cnt_ref, run_ref):
    ya = _dot(oa_ref[...], wpa_ref[...])
    yb = _dot(ob_ref[...], wpb_ref[...])
    merged = jax.nn.sigmoid(ga_ref[...].astype(F32)) * ya + jax.nn.sigmoid(gb_ref[...].astype(F32)) * yb
    pre = DEEPNORM_ALPHA * x_ref[...] + _dot(merged.astype(BF16), wout_ref[...])
    h = _layer_norm(pre, g1_ref[...], b1_ref[...])
    _store_row_tiles(h_ref, h)
    hb = h.astype(BF16)
    scores = jax.nn.sigmoid(_dot(hb, wr_ref[...]))
    up = _dot(hb, wsu_ref[...])
    act = _silu(up[:, :SHARED_FF]) * up[:, SHARED_FF:]
    hs_ref[...] = DEEPNORM_ALPHA * h + _dot(act.astype(BF16), wsd_ref[...])
    _route_tile(scores, rbias_ref, idx_ref, gate_ref, rank_ref, cnt_ref, run_ref)


def _merge(oa, ob, ga, gb, xf, wpa, wpb, wout, wr, wsu, wsd, g1, b1, rbias, tm=512):
    T = xf.shape[0]
    row = lambda w: pl.BlockSpec((tm, w), lambda i: (i, 0))
    return pl.pallas_call(
        _merge_kernel,
        out_shape=[jax.ShapeDtypeStruct((T * ROW_TILE, LANES), U32), jax.ShapeDtypeStruct((T, D_MODEL), F32),
                   jax.ShapeDtypeStruct((T, LANES), I32), jax.ShapeDtypeStruct((T, LANES), F32),
                   jax.ShapeDtypeStruct((T, LANES), I32), jax.ShapeDtypeStruct((1, N_EXPERTS), F32)],
        grid=(T // tm,),
        in_specs=[row(DN_WIDTH), row(SB_WIDTH), row(D_MODEL), row(D_MODEL), row(D_MODEL),
                  _const_spec(wpa.shape), _const_spec(wpb.shape), _const_spec(wout.shape), _const_spec(wr.shape),
                  _const_spec(wsu.shape), _const_spec(wsd.shape), _const_spec(g1.shape), _const_spec(b1.shape),
                  _const_spec(rbias.shape)],
        out_specs=[pl.BlockSpec((tm * ROW_TILE, LANES), lambda i: (i, 0)), row(D_MODEL), row(LANES), row(LANES),
                   row(LANES), pl.BlockSpec((1, N_EXPERTS), lambda i: (0, 0))],
        scratch_shapes=[pltpu.VMEM((1, N_EXPERTS), F32)],
        compiler_params=_params(("arbitrary",)),
        name="merge_norm_shared_route",
    )(oa, ob, ga, gb, xf, wpa, wpb, wout, wr, wsu, wsd, g1, b1, rbias)


def _route_tile(scores, bias_ref, idx_ref, gate_ref, rank_ref, cnt_ref, run_ref):
    i = pl.program_id(0)
    tm = scores.shape[0]

    @pl.when(i == 0)
    def _():
        run_ref[...] = jnp.zeros_like(run_ref)

    sel = scores + bias_ref[...]
    lane_i = lax.broadcasted_iota(I32, (tm, N_EXPERTS), 1)
    grp = lane_i // GROUP_SIZE
    lane = lane_i.astype(F32)
    neg = F32(-jnp.inf)

    def first_argmax(v):
        m = jnp.max(v, -1, keepdims=True)
        ix = jnp.min(jnp.where(v == m, lane, float(N_EXPERTS)), -1, keepdims=True)
        return m, ix

    gscore = []
    for g in range(N_GROUPS):
        gv = jnp.where(grp == g, sel, neg)
        m1, i1 = first_argmax(gv)
        m2 = jnp.max(jnp.where(lane == i1, neg, gv), -1, keepdims=True)
        gscore.append(m1 + m2)
    emask = jnp.zeros((tm, N_EXPERTS), jnp.bool_)
    for g in range(N_GROUPS):
        beaten = jnp.zeros((tm, 1), I32)
        for o in range(N_GROUPS):
            if o == g:
                continue
            wins = (gscore[o] > gscore[g]) | ((gscore[o] == gscore[g]) & (o < g))
            beaten = beaten + wins.astype(I32)
        emask = emask | ((grp == g) & (beaten < TOP_GROUPS))
    masked = jnp.where(emask, sel, neg)

    picked = jnp.zeros((tm, N_EXPERTS), jnp.bool_)
    idx_cols, score_cols, hots = [], [], []
    for _ in range(TOP_K):
        _, ix = first_argmax(masked)
        hot = lane == ix
        idx_cols.append(ix)
        score_cols.append(jnp.sum(jnp.where(hot, scores, 0.0), -1, keepdims=True))
        hots.append(hot)
        masked = jnp.where(hot, neg, masked)
        picked = picked | hot
    chosen = jnp.where(picked, 1.0, 0.0).astype(BF16)
    r_io = lax.broadcasted_iota(I32, (tm, tm), 0)
    c_io = lax.broadcasted_iota(I32, (tm, tm), 1)
    strict_lower = jnp.where(c_io < r_io, 1.0, 0.0).astype(BF16)
    before = _dot(strict_lower, chosen) + run_ref[...]
    run_ref[...] = run_ref[...] + jnp.sum(chosen.astype(F32), 0, keepdims=True)
    cnt_ref[...] = run_ref[...]

    total = score_cols[0]
    for s in score_cols[1:]:
        total = total + s
    lane_o = lax.broadcasted_iota(I32, (tm, LANES), 1)
    idx_o = jnp.zeros((tm, LANES), F32)
    gate_o = jnp.zeros((tm, LANES), F32)
    rank_o = jnp.zeros((tm, LANES), F32)
    for k in range(TOP_K):
        rk = jnp.sum(jnp.where(hots[k], before, 0.0), -1, keepdims=True)
        idx_o = jnp.where(lane_o == k, idx_cols[k], idx_o)
        gate_o = jnp.where(lane_o == k, score_cols[k] / total * ROUTED_SCALE, gate_o)
        rank_o = jnp.where(lane_o == k, rk, rank_o)
    idx_ref[...] = idx_o.astype(I32)
    gate_ref[...] = gate_o
    rank_ref[...] = rank_o.astype(I32)


def _dest_kernel(idx_ref, rank_ref, pstart_ref, dest_ref):
    tm = idx_ref.shape[0]
    idx = idx_ref[...]
    lane = lax.broadcasted_iota(I32, (tm, N_EXPERTS), 1)
    lane_o = lax.broadcasted_iota(I32, (tm, LANES), 1)
    start = jnp.zeros((tm, LANES), F32)
    for k in range(TOP_K):
        sk = jnp.sum(jnp.where(lane == idx[:, k:k + 1], pstart_ref[...], 0.0), -1, keepdims=True)
        start = jnp.where(lane_o == k, sk, start)
    dest_ref[...] = start.astype(I32) + rank_ref[...]


def _dest_rows(idx, rank, pstart_f, tm=512):
    T = idx.shape[0]
    row = pl.BlockSpec((tm, LANES), lambda i: (i, 0))
    return pl.pallas_call(
        _dest_kernel,
        out_shape=jax.ShapeDtypeStruct((T, LANES), I32),
        grid=(T // tm,),
        in_specs=[row, row, pl.BlockSpec((1, N_EXPERTS), lambda i: (0, 0))],
        out_specs=row,
        compiler_params=_params(("arbitrary",)),
        name="dest_rows",
    )(idx, rank, pstart_f)


DISPATCH_TOKENS = 512
COMBINE_TOKENS = 256
EXPERT_ROWS = 256


def _dispatch_kernel(dest_ref, h_ref, xs_hbm, sem):
    def issue(t, carry):
        src = pl.multiple_of(t * ROW_TILE, ROW_TILE)
        for k in range(TOP_K):
            dst = pl.multiple_of(dest_ref[t * TOP_K + k] * ROW_TILE, ROW_TILE)
            pltpu.make_async_copy(h_ref.at[pl.ds(src, ROW_TILE)], xs_hbm.at[pl.ds(dst, ROW_TILE)],
                                  sem).start(priority=k % 2)
        return carry

    lax.fori_loop(0, DISPATCH_TOKENS, issue, 0)
    for _ in range(TOP_K):
        pltpu.make_async_copy(h_ref, xs_hbm.at[pl.ds(0, DISPATCH_TOKENS * ROW_TILE)], sem).wait()


def _dispatch(dest_flat, h_rt, n_rows):
    T = h_rt.shape[0] // ROW_TILE
    tm = DISPATCH_TOKENS
    return pl.pallas_call(
        _dispatch_kernel,
        out_shape=jax.ShapeDtypeStruct((n_rows * ROW_TILE, LANES), U32),
        grid=(T // tm,),
        in_specs=[pl.BlockSpec((tm * TOP_K,), lambda i: (i,), memory_space=pltpu.SMEM),
                  pl.BlockSpec((tm * ROW_TILE, LANES), lambda i: (i, 0))],
        out_specs=pl.BlockSpec(memory_space=pl.ANY),
        scratch_shapes=[pltpu.SemaphoreType.DMA(())],
        compiler_params=pltpu.CompilerParams(dimension_semantics=("arbitrary",), has_side_effects=True,
                                             vmem_limit_bytes=VMEM_LIMIT),
        name="dispatch_rows",
    )(dest_flat, h_rt)


def _expert_kernel(blk_ref, be_ref, first_ref, slot_ref, nexte_ref, nu_ref, xs_ref, wu_hbm, wd_hbm, ys_ref,
                   wu32_ref, wd32_ref, wu16_ref, wd16_ref, sems):
    b = pl.program_id(0)
    used = b < nu_ref[0]

    def weight_copies(e, s):
        return (pltpu.make_async_copy(wu_hbm.at[e], wu32_ref.at[s], sems.at[0, s]),
                pltpu.make_async_copy(wd_hbm.at[e], wd32_ref.at[s], sems.at[1, s]))

    @pl.when(b == 0)
    def _():
        for c in weight_copies(be_ref[0], 0):
            c.start()

    @pl.when(used & (first_ref[b] == 1))
    def _():
        s = slot_ref[b]
        for c in weight_copies(be_ref[b], s):
            c.wait()
        wu16_ref[...] = wu32_ref[s].astype(BF16)
        wd16_ref[...] = wd32_ref[s].astype(BF16)

        @pl.when(nexte_ref[b] >= 0)
        def _():
            for c in weight_copies(nexte_ref[b], 1 - s):
                c.start()

    @pl.when(used)
    def _():
        xb = _load_row_tiles(xs_ref, EXPERT_ROWS).astype(BF16)
        up = _dot(xb, wu16_ref[...])
        act = _silu(up[:, :EXPERT_FF]) * up[:, EXPERT_FF:]
        _store_row_tiles(ys_ref, _dot(act.astype(BF16), wd16_ref[...]))


def _experts(block_idx, block_e, first, slot, next_e, n_used, xs, w_up, w_down):
    n_blocks = xs.shape[0] // (EXPERT_ROWS * ROW_TILE)
    rows = pl.BlockSpec((EXPERT_ROWS * ROW_TILE, LANES), lambda b, bi, *_: (bi[b], 0))
    grid_spec = pltpu.PrefetchScalarGridSpec(
        num_scalar_prefetch=6,
        grid=(n_blocks,),
        in_specs=[rows, pl.BlockSpec(memory_space=pl.ANY), pl.BlockSpec(memory_space=pl.ANY)],
        out_specs=rows,
        scratch_shapes=[pltpu.VMEM((2, D_MODEL, 2 * EXPERT_FF), F32), pltpu.VMEM((2, EXPERT_FF, D_MODEL), F32),
                        pltpu.VMEM((D_MODEL, 2 * EXPERT_FF), BF16), pltpu.VMEM((EXPERT_FF, D_MODEL), BF16),
                        pltpu.SemaphoreType.DMA((2, 2))],
    )
    return pl.pallas_call(
        _expert_kernel,
        out_shape=jax.ShapeDtypeStruct(xs.shape, U32),
        grid_spec=grid_spec,
        compiler_params=_params(("arbitrary",)),
        name="routed_experts",
    )(block_idx, block_e, first, slot, next_e, n_used, xs, w_up, w_down)


def _combine_kernel(dest_ref, destn_ref, gate_ref, hs_ref, g2_ref, b2_ref, ys_hbm, out_ref, ybuf, sems):
    i = pl.program_id(0)
    n = pl.num_programs(0)
    slot = i % 2

    def gather(dr, s):
        def issue(t, carry):
            dst = pl.multiple_of(t * ROW_TILE, ROW_TILE)
            for k in range(TOP_K):
                src = pl.multiple_of(dr[t * TOP_K + k] * ROW_TILE, ROW_TILE)
                pltpu.make_async_copy(ys_hbm.at[pl.ds(src, ROW_TILE)], ybuf.at[s, k, pl.ds(dst, ROW_TILE)],
                                      sems.at[s]).start(priority=k % 2)
            return carry

        lax.fori_loop(0, COMBINE_TOKENS, issue, 0)

    @pl.when(i == 0)
    def _():
        gather(dest_ref, 0)

    @pl.when(i + 1 < n)
    def _():
        gather(destn_ref, 1 - slot)

    for k in range(TOP_K):
        pltpu.make_async_copy(ys_hbm.at[pl.ds(0, COMBINE_TOKENS * ROW_TILE)], ybuf.at[slot, k],
                              sems.at[slot]).wait()

    gates = gate_ref[...]
    acc = hs_ref[...]
    for k in range(TOP_K):
        acc = acc + gates[:, k:k + 1] * _load_row_tiles(ybuf, COMBINE_TOKENS, (slot, k))
    out_ref[...] = _layer_norm(acc, g2_ref[...], b2_ref[...])


def _combine(dest_flat, gates, hs, g2, b2, ys):
    T = hs.shape[0]
    tm = COMBINE_TOKENS
    n = T // tm
    cur = pl.BlockSpec((tm * TOP_K,), lambda i: (i,), memory_space=pltpu.SMEM)
    nxt = pl.BlockSpec((tm * TOP_K,), lambda i: (jnp.minimum(i + 1, n - 1),), memory_space=pltpu.SMEM)
    return pl.pallas_call(
        _combine_kernel,
        out_shape=jax.ShapeDtypeStruct((T, D_MODEL), F32),
        grid=(n,),
        in_specs=[cur, nxt, pl.BlockSpec((tm, LANES), lambda i: (i, 0)), pl.BlockSpec((tm, D_MODEL), lambda i: (i, 0)),
                  pl.BlockSpec((1, D_MODEL), lambda i: (0, 0)), pl.BlockSpec((1, D_MODEL), lambda i: (0, 0)),
                  pl.BlockSpec(memory_space=pl.ANY)],
        out_specs=pl.BlockSpec((tm, D_MODEL), lambda i: (i, 0)),
        scratch_shapes=[pltpu.VMEM((2, TOP_K, tm * ROW_TILE, LANES), U32), pltpu.SemaphoreType.DMA((2,))],
        compiler_params=_params(("arbitrary",)),
        name="combine_norm",
    )(dest_flat, dest_flat, gates, hs, g2, b2, ys)


def _pack_w_in(w_in):
    c1 = 4 * DN_WIDTH
    c2 = c1 + 2 * DN_HEADS
    ba = jnp.pad(w_in[:, c1:c2], ((0, 0), (0, LANES - 2 * DN_HEADS)))
    return jnp.concatenate([w_in[:, :c1], ba, w_in[:, c2:]], axis=1).astype(BF16)


def _head_lanes(v):
    return jnp.pad(v.astype(F32), (DN_HEADS, LANES - 2 * DN_HEADS)).reshape(1, LANES)


def _layer(x, w_in, conv_w, dn_a_log, dn_dt_bias, dn_norm_g, w_proj_a, w_proj_b, w_out, ln1_g, ln1_b, w_router,
           router_bias, w_shared_up, w_shared_down, w_expert_up, w_expert_down, ln2_g, ln2_b):
    B, S, D = x.shape
    T = B * S
    xf = x.reshape(T, D)
    qkv, z, ba, sq, sk, sv, ga, gb = _input_projection(xf, _pack_w_in(w_in))

    oa = _deltanet(qkv, z, ba, conv_w, _head_lanes(dn_a_log), _head_lanes(dn_dt_bias),
                   dn_norm_g.reshape(1, DN_HEAD_DIM), B, S)
    ob = _stick_breaking(sq, sk, sv, B, S)

    h, hs, idx, gates, rank, counts = _merge(
        oa, ob, ga, gb, xf, w_proj_a.astype(BF16), w_proj_b.astype(BF16), w_out.astype(BF16), w_router.astype(BF16),
        w_shared_up.astype(BF16), w_shared_down.astype(BF16), ln1_g.reshape(1, D), ln1_b.reshape(1, D),
        router_bias.reshape(1, N_EXPERTS).astype(F32))

    out = _routed_path(h, hs, idx, gates, rank, counts, w_expert_up, w_expert_down, ln2_g.reshape(1, D),
                       ln2_b.reshape(1, D))
    return out.reshape(B, S, D)


def _routed_path(h, hs, idx, gates, rank, counts, w_expert_up, w_expert_down, g2, b2):
    T = hs.shape[0]
    counts = counts.reshape(N_EXPERTS).astype(I32)
    padded = (counts + EXPERT_ROWS - 1) // EXPERT_ROWS * EXPERT_ROWS
    pend = jnp.cumsum(padded)
    pstart = (pend - padded).astype(I32)
    n_blocks = -(-(T * TOP_K + N_EXPERTS * (EXPERT_ROWS - 1)) // EXPERT_ROWS)
    n_used = (pend[-1] // EXPERT_ROWS).astype(I32)
    block_idx = jnp.minimum(jnp.arange(n_blocks, dtype=I32), n_used - 1)
    block_e = jnp.minimum(jnp.sum(pend[None, :] <= (block_idx * EXPERT_ROWS)[:, None], 1), N_EXPERTS - 1).astype(I32)

    prev_e = jnp.concatenate([jnp.full((1,), -1, I32), block_e[:-1]])
    first = (block_e != prev_e).astype(I32)
    slot = ((jnp.cumsum(first) - 1) % 2).astype(I32)
    expert_ids = jnp.arange(N_EXPERTS, dtype=I32)
    later_used = jnp.where(counts > 0, expert_ids, N_EXPERTS)
    next_used = lax.cummin(jnp.concatenate([later_used[1:], jnp.full((1,), N_EXPERTS, I32)]), reverse=True)
    next_used = jnp.where(next_used < N_EXPERTS, next_used, -1)
    next_e = next_used[block_e].astype(I32)

    dest = _dest_rows(idx, rank, pstart.astype(F32).reshape(1, N_EXPERTS))
    dest_flat = dest[:, :TOP_K].reshape(T * TOP_K)
    xs = _dispatch(dest_flat, h, n_blocks * EXPERT_ROWS)
    ys = _experts(block_idx, block_e, first, slot, next_e, n_used.reshape(1), xs, w_expert_up, w_expert_down)
    return _combine(dest_flat, gates, hs, g2, b2, ys)


def kernel(x, w_in, conv_w, dn_a_log, dn_dt_bias, dn_norm_g, w_proj_a, w_proj_b, w_out, ln1_g, ln1_b, w_router,
           router_bias, w_shared_up, w_shared_down, w_expert_up, w_expert_down, ln2_g, ln2_b):
    for layer in range(DEPTH):
        x = _layer(x, w_in[layer], conv_w[layer], dn_a_log[layer], dn_dt_bias[layer], dn_norm_g[layer],
                   w_proj_a[layer], w_proj_b[layer], w_out[layer], ln1_g[layer], ln1_b[layer], w_router[layer],
                   router_bias[layer], w_shared_up[layer], w_shared_down[layer], w_expert_up[layer],
                   w_expert_down[layer], ln2_g[layer], ln2_b[layer])
    return x
```
